```python
import jax, jax.numpy as jnp
from jax import lax
import numpy as np

D_MODEL = 1024
BATCH = 16
SEQ = 2048
DEPTH = 2

PLE_DIM = 256
CHUNK = 128
SG_GROUPS = 4
SG_GROUP_DIM = 128
SG_WIDTH = SG_GROUPS * SG_GROUP_DIM
SB_HEADS = 8
SB_HEAD_DIM = 64
SB_WIDTH = SB_HEADS * SB_HEAD_DIM
AB_IN = 2 * SG_WIDTH + 3 * SB_WIDTH
AB_OUT = SG_WIDTH + SB_WIDTH
RET_HEADS = 4
RET_DK = D_MODEL // RET_HEADS
RET_DV = 2 * RET_DK
RET_QK = RET_HEADS * RET_DK
RET_V = RET_HEADS * RET_DV
RET_IN = 2 * RET_QK + 2 * RET_V
ROPE_BASE = 10000.0
D_FF = 2816
CONV_W = 3
RMS_EPS = 1e-6
LN_EPS = 1e-5
N_EVEN = (DEPTH + 1) // 2
N_ODD = DEPTH // 2

kernel_name = 'hybrid_sgmlp_stickbreak_retention_trunk'


def rms_norm(x, g):
    xf = x.astype(jnp.float32)
    y = xf * lax.rsqrt(jnp.mean(xf * xf, axis=-1, keepdims=True) + RMS_EPS)
    return (y * g.astype(jnp.float32)).astype(x.dtype)


def spatial_gating(u, v, ln_g, ln_b, w_s, b_s):
    Bn, S, _ = u.shape
    nc = S // CHUNK
    vf = v.astype(jnp.float32).reshape(Bn, nc, CHUNK, SG_GROUPS, SG_GROUP_DIM)
    mean = jnp.mean(vf, axis=-1, keepdims=True)
    var = jnp.mean(jnp.square(vf - mean), axis=-1, keepdims=True)
    g = ln_g.astype(jnp.float32).reshape(SG_GROUPS, SG_GROUP_DIM)
    b = ln_b.astype(jnp.float32).reshape(SG_GROUPS, SG_GROUP_DIM)
    vn = (vf - mean) * lax.rsqrt(var + LN_EPS) * g + b
    causal = jnp.tril(jnp.ones((CHUNK, CHUNK), dtype=bool))
    w = jnp.where(causal[None], w_s.astype(jnp.float32), 0.0)
    mixed = jnp.einsum('gts,bnsgc->bntgc', w, vn) + jnp.transpose(b_s.astype(jnp.float32))[None, None, :, :, None]
    out = u.reshape(Bn, nc, CHUNK, SG_GROUPS, SG_GROUP_DIM) * mixed.astype(u.dtype)
    return out.reshape(Bn, S, SG_WIDTH)


def stick_breaking_attention(q, k, v):
    Bn, S, H, Dh = q.shape
    qf = q.astype(jnp.float32) * (Dh ** -0.5)
    kf = k.astype(jnp.float32)
    vf = v.astype(jnp.float32)
    outs = []
    for i in range(S // CHUNK):
        q0 = i * CHUNK
        kv_len = q0 + CHUNK
        z = jnp.einsum('bthd,bshd->bhts', qf[:, q0:kv_len], kf[:, :kv_len])
        t_pos = q0 + jnp.arange(CHUNK)[:, None]
        s_pos = jnp.arange(kv_len)[None, :]
        causal = s_pos < t_pos
        neg_log_1m_beta = jnp.where(causal, jax.nn.softplus(z), 0.0)
        later = lax.cumsum(neg_log_1m_beta, axis=3, reverse=True) - neg_log_1m_beta
        log_a = jax.nn.log_sigmoid(z) - later
        a = jnp.where(causal, jnp.exp(log_a), 0.0)
        outs.append(jnp.einsum('bhts,bshd->bthd', a, vf[:, :kv_len]))
    return jnp.concatenate(outs, axis=1).astype(q.dtype)


def hybrid_sg_sb_mixer(h, w_in, ln_g, ln_b, w_s, b_s, w_out):
    Bn, S, _ = h.shape
    proj = h @ w_in
    u, v, q, k, vv = jnp.split(proj, [SG_WIDTH, 2 * SG_WIDTH, 2 * SG_WIDTH + SB_WIDTH, 2 * SG_WIDTH + 2 * SB_WIDTH], axis=-1)
    a_out = spatial_gating(jax.nn.gelu(u, approximate=False), jax.nn.gelu(v, approximate=False), ln_g, ln_b, w_s, b_s)
    q = q.reshape(Bn, S, SB_HEADS, SB_HEAD_DIM)
    k = k.reshape(Bn, S, SB_HEADS, SB_HEAD_DIM)
    vv = vv.reshape(Bn, S, SB_HEADS, SB_HEAD_DIM)
    b_out = stick_breaking_attention(q, k, vv).reshape(Bn, S, SB_WIDTH)
    return jnp.concatenate([a_out, b_out], axis=-1) @ w_out


def rotary(x):
    S = x.shape[1]
    half = x.shape[-1] // 2
    inv = 1.0 / (ROPE_BASE ** (jnp.arange(half, dtype=jnp.float32) / half))
    ang = jnp.arange(S, dtype=jnp.float32)[:, None] * inv[None, :]
    cos = jnp.cos(ang)[None, :, None, :]
    sin = jnp.sin(ang)[None, :, None, :]
    xf = x.astype(jnp.float32)
    x1, x2 = xf[..., :half], xf[..., half:]
    return jnp.concatenate([x1 * cos - x2 * sin, x2 * cos + x1 * sin], axis=-1)


def retention(q, k, v):
    Bn, S, H, Dk = q.shape
    Dv = v.shape[-1]
    nc = S // CHUNK
    log_gamma = jnp.log(1.0 - 2.0 ** (-5.0 - jnp.arange(H, dtype=jnp.float32)))
    idx = jnp.arange(CHUNK, dtype=jnp.float32)
    diff = idx[:, None] - idx[None, :]
    decay_intra = jnp.where(diff[None] >= 0, jnp.exp(diff[None] * log_gamma[:, None, None]), 0.0)
    zeta = jnp.exp((CHUNK - 1 - idx)[None, :] * log_gamma[:, None])
    xi = jnp.exp((idx + 1.0)[None, :] * log_gamma[:, None])
    chunk_decay = jnp.exp(CHUNK * log_gamma)

    def to_chunks(t):
        return t.astype(jnp.float32).reshape(Bn, nc, CHUNK, H, t.shape[-1]).transpose(1, 0, 3, 2, 4)

    qc, kc, vc = to_chunks(q), to_chunks(k), to_chunks(v)

    def step(state, inp):
        qi, ki, vi = inp
        inner = jnp.einsum('bhtd,bhsd->bhts', qi, ki) * decay_intra[None]
        o = jnp.einsum('bhts,bhsv->bhtv', inner, vi) + jnp.einsum('bhtd,bhdv->bhtv', qi, state) * xi[None, :, :, None]
        state = state * chunk_decay[None, :, None, None] + jnp.einsum('bhsd,bhsv->bhdv', ki * zeta[None, :, :, None], vi)
        return state, o

    state0 = jnp.zeros((Bn, H, Dk, Dv), jnp.float32)
    _, o = lax.scan(step, state0, (qc, kc, vc))
    return o.transpose(1, 0, 3, 2, 4).reshape(Bn, S, H, Dv)


def retention_mixer(h, w_in, gn_g, w_out):
    Bn, S, _ = h.shape
    q, k, v, g = jnp.split(h @ w_in, [RET_QK, 2 * RET_QK, 2 * RET_QK + RET_V], axis=-1)
    q = rotary(q.reshape(Bn, S, RET_HEADS, RET_DK))
    k = rotary(k.reshape(Bn, S, RET_HEADS, RET_DK)) * (RET_DK ** -0.5)
    o = retention(q, k, v.reshape(Bn, S, RET_HEADS, RET_DV))
    mean = jnp.mean(o, axis=-1, keepdims=True)
    var = jnp.mean(jnp.square(o - mean), axis=-1, keepdims=True)
    o = ((o - mean) * lax.rsqrt(var + LN_EPS)).reshape(Bn, S, RET_V) * gn_g.astype(jnp.float32)
    return (jax.nn.silu(g) * o.astype(h.dtype)) @ w_out


def conv_ffn(h, w_up, conv_w, conv_b, w_down):
    S = h.shape[1]
    a = h @ w_up
    ap = jnp.pad(a, ((0, 0), (CONV_W - 1, 0), (0, 0)))
    c = conv_b + sum(ap[:, j:j + S] * conv_w[j] for j in range(CONV_W))
    gate, up = jnp.split(c, 2, axis=-1)
    return (jax.nn.gelu(gate, approximate=False) * up) @ w_down


def setup_inputs(seed: int = 0) -> dict:
    key = jax.random.key(seed)
    ks = jax.random.split(key, 24)
    f32 = jnp.float32

    def nrm(k, shape, scale):
        return jax.random.normal(k, shape, f32) * scale

    def gain(k, shape):
        return 1.0 + 0.05 * jax.random.normal(k, shape, f32)

    return {
        'x': nrm(ks[0], (BATCH, SEQ, D_MODEL), 1.0),
        'p': nrm(ks[1], (DEPTH, BATCH, SEQ, PLE_DIM), 1.0),
        'mix_norm_g': gain(ks[2], (DEPTH, D_MODEL)),
        'ffn_norm_g': gain(ks[3], (DEPTH, D_MODEL)),
        'ple_norm_g': gain(ks[4], (DEPTH, D_MODEL)),
        'ab_w_in': nrm(ks[5], (N_EVEN, D_MODEL, AB_IN), D_MODEL ** -0.5),
        'sg_ln_g': gain(ks[6], (N_EVEN, SG_WIDTH)),
        'sg_ln_b': nrm(ks[7], (N_EVEN, SG_WIDTH), 0.02),
        'sg_w': nrm(ks[8], (N_EVEN, SG_GROUPS, CHUNK, CHUNK), CHUNK ** -0.5),
        'sg_b': 1.0 + nrm(ks[9], (N_EVEN, SG_GROUPS, CHUNK), 0.1),
        'ab_w_out': nrm(ks[10], (N_EVEN, AB_OUT, D_MODEL), AB_OUT ** -0.5),
        'ret_w_in': nrm(ks[11], (N_ODD, D_MODEL, RET_IN), D_MODEL ** -0.5),
        'ret_gn_g': gain(ks[12], (N_ODD, RET_V)),
        'ret_w_out': nrm(ks[13], (N_ODD, RET_V, D_MODEL), RET_V ** -0.5),
        'ffn_w_up': nrm(ks[14], (DEPTH, D_MODEL, 2 * D_FF), D_MODEL ** -0.5),
        'ffn_conv_w': nrm(ks[15], (DEPTH, CONV_W, 2 * D_FF), CONV_W ** -0.5),
        'ffn_conv_b': nrm(ks[16], (DEPTH, 2 * D_FF), 0.02),
        'ffn_w_down': nrm(ks[17], (DEPTH, D_FF, D_MODEL), D_FF ** -0.5),
        'ple_w_gate': nrm(ks[18], (DEPTH, D_MODEL, D_MODEL), D_MODEL ** -0.5),
        'ple_w_proj': nrm(ks[19], (DEPTH, PLE_DIM, D_MODEL), PLE_DIM ** -0.5),
        'ple_post_g': gain(ks[20], (DEPTH, D_MODEL)),
        'final_norm_g': gain(ks[21], (D_MODEL,)),
    }


def reference(x, p, mix_norm_g, ffn_norm_g, ple_norm_g, ab_w_in, sg_ln_g, sg_ln_b, sg_w, sg_b, ab_w_out,
              ret_w_in, ret_gn_g, ret_w_out, ffn_w_up, ffn_conv_w, ffn_conv_b, ffn_w_down,
              ple_w_gate, ple_w_proj, ple_post_g, final_norm_g):
    for i in range(DEPTH):
        h = rms_norm(x, mix_norm_g[i])
        if i % 2 == 0:
            e = i // 2
            x = x + hybrid_sg_sb_mixer(h, ab_w_in[e], sg_ln_g[e], sg_ln_b[e], sg_w[e], sg_b[e], ab_w_out[e])
        else:
            o = i // 2
            x = x + retention_mixer(h, ret_w_in[o], ret_gn_g[o], ret_w_out[o])
        x = x + conv_ffn(rms_norm(x, ffn_norm_g[i]), ffn_w_up[i], ffn_conv_w[i], ffn_conv_b[i], ffn_w_down[i])
        gate = jax.nn.sigmoid(rms_norm(x, ple_norm_g[i]) @ ple_w_gate[i])
        x = x + gate * rms_norm(p[i] @ ple_w_proj[i], ple_post_g[i])
    return rms_norm(x, final_norm_g)
```

```python
import functools

import jax
import jax.numpy as jnp
from jax import lax
from jax.experimental import pallas as pl
from jax.experimental.pallas import tpu as pltpu

F32 = jnp.float32
BF16 = jnp.bfloat16

D_MODEL = 1024
BATCH = 16
SEQ = 2048
TOKENS = BATCH * SEQ
PLE_DIM = 256
CHUNK = 128
SG_GROUPS = 4
SG_WIDTH = 512
SB_HEADS = 8
SB_HEAD_DIM = 64
SB_WIDTH = 512
RET_HEADS = 4
RET_DK = 256
RET_DV = 512
RET_QK = 1024
RET_V = 2048
RET_IN = 6144
ROPE_BASE = 10000.0
D_FF = 2816
RMS_EPS = 1e-6
LN_EPS = 1e-5

LANES = 128
SUBLANES = 8
VMEM_LIMIT_BYTES = 56 * 1024 * 1024

ROW_TILE = 512
FF_CHUNK = 256
N_FF_CHUNKS = D_FF // FF_CHUNK
SB_TQ = 256
SB_TK = 128


def _resident(shape):
    nd = len(shape)
    return pl.BlockSpec(shape, lambda *_: (0,) * nd, pipeline_mode=pl.Buffered(1))


def _params(semantics):
    return pltpu.CompilerParams(dimension_semantics=semantics, vmem_limit_bytes=VMEM_LIMIT_BYTES)


def _rms(x, g):
    return x * lax.rsqrt(jnp.mean(x * x, axis=-1, keepdims=True) + RMS_EPS) * g


def _gelu(x):
    return 0.5 * x * (1.0 + lax.erf(x * (0.5 ** 0.5)))


def _sigmoid(x):
    return 1.0 / (1.0 + jnp.exp(-x))


def _mix0_in_kernel(x_ref, g_ref, wuv_ref, wqkv_ref, lng_ref, lnb_ref, ws_ref, bs_ref, a_ref, qkv_ref):
    h = _rms(x_ref[...], g_ref[...]).astype(BF16)
    uv = _gelu(jnp.dot(h, wuv_ref[...], preferred_element_type=F32))
    qkv = jnp.dot(h, wqkv_ref[...], preferred_element_type=F32)
    qkv_ref[:, :SB_WIDTH] = (qkv[:, :SB_WIDTH] * (SB_HEAD_DIM ** -0.5)).astype(BF16)
    qkv_ref[:, SB_WIDTH:] = qkv[:, SB_WIDTH:].astype(BF16)

    n_chunks = ROW_TILE // CHUNK
    row = lax.broadcasted_iota(jnp.int32, (CHUNK, CHUNK), 0)
    col = lax.broadcasted_iota(jnp.int32, (CHUNK, CHUNK), 1)
    for g in range(SG_GROUPS):
        gs = slice(g * LANES, (g + 1) * LANES)
        w = jnp.where(row >= col, ws_ref[g], 0.0).astype(BF16)
        vn = []
        for c in range(n_chunks):
            vf = uv[c * CHUNK:(c + 1) * CHUNK, SG_WIDTH + g * LANES:SG_WIDTH + (g + 1) * LANES]
            d = vf - jnp.mean(vf, axis=-1, keepdims=True)
            var = jnp.mean(d * d, axis=-1, keepdims=True)
            vn.append((d * lax.rsqrt(var + LN_EPS) * lng_ref[:, gs] + lnb_ref[:, gs]).astype(BF16))
        mixed = jnp.dot(w, jnp.concatenate(vn, axis=1), preferred_element_type=F32)
        for c in range(n_chunks):
            m = mixed[:, c * LANES:(c + 1) * LANES] + bs_ref[g]
            a_ref[c * CHUNK:(c + 1) * CHUNK, gs] = (uv[c * CHUNK:(c + 1) * CHUNK, gs] * m).astype(BF16)


def _mix0_in(x, g, wuv, wqkv, lng, lnb, ws, bs):
    tm = ROW_TILE
    return pl.pallas_call(
        _mix0_in_kernel,
        grid=(TOKENS // tm,),
        in_specs=[
            pl.BlockSpec((tm, D_MODEL), lambda i: (i, 0)),
            _resident((1, D_MODEL)),
            _resident((D_MODEL, 2 * SG_WIDTH)),
            _resident((D_MODEL, 3 * SB_WIDTH)),
            _resident((1, SG_WIDTH)),
            _resident((1, SG_WIDTH)),
            _resident((SG_GROUPS, CHUNK, CHUNK)),
            _resident((SG_GROUPS, CHUNK, LANES)),
        ],
        out_specs=[
            pl.BlockSpec((tm, SG_WIDTH), lambda i: (i, 0)),
            pl.BlockSpec((tm, 3 * SB_WIDTH), lambda i: (i, 0)),
        ],
        out_shape=[
            jax.ShapeDtypeStruct((TOKENS, SG_WIDTH), BF16),
            jax.ShapeDtypeStruct((TOKENS, 3 * SB_WIDTH), BF16),
        ],
        compiler_params=_params(("arbitrary",)),
        name="mix0_in",
    )(x, g, wuv, wqkv, lng, lnb, ws, bs)


def _sb_kernel(q_ref, k_ref, v_ref, w2_ref, o_ref):
    i = pl.program_id(2)
    lane = lax.broadcasted_iota(jnp.int32, (1, LANES), 1)
    first_head = lane < SB_HEAD_DIM
    q = q_ref[...]
    w2 = w2_ref[...]
    row = lax.broadcasted_iota(jnp.int32, (SB_TQ, SB_TK), 0)
    col = lax.broadcasted_iota(jnp.int32, (SB_TQ, SB_TK), 1)

    def split_heads(blk):
        zero = jnp.zeros_like(blk)
        return jnp.concatenate([jnp.where(first_head, blk, zero), jnp.where(first_head, zero, blk)], axis=0)

    def step(j, carry, diag_offset):
        c0, c1, acc = carry
        r0 = pl.multiple_of(j * SB_TK, SB_TK)
        kcat = split_heads(k_ref[pl.ds(r0, SB_TK), :])
        vcat = split_heads(v_ref[pl.ds(r0, SB_TK), :])
        z = lax.dot_general(q, kcat, (((1,), (1,)), ((), ())), preferred_element_type=F32)
        causal = None if diag_offset is None else (col + diag_offset) < row
        a_parts, new_c = [], []
        for hd, c in enumerate((c0, c1)):
            zh = z[:, hd * SB_TK:(hd + 1) * SB_TK]
            sp = jnp.maximum(zh, 0.0) + jnp.log1p(jnp.exp(-jnp.abs(zh)))
            spm = sp if causal is None else jnp.where(causal, sp, 0.0)
            hi = spm.astype(BF16)
            lo = (spm - hi.astype(F32)).astype(BF16)
            cs = jnp.dot(jnp.concatenate([hi, lo], axis=1), w2, preferred_element_type=F32)
            log_a = zh - sp - (c + cs[:, :SB_TK])
            a = jnp.exp(log_a)
            if causal is not None:
                a = jnp.where(causal, a, 0.0)
            a_parts.append(a.astype(BF16))
            new_c.append(c + cs[:, SB_TK:])
        acc = acc + jnp.dot(jnp.concatenate(a_parts, axis=1), vcat, preferred_element_type=F32)
        return new_c[0], new_c[1], acc

    zeros = jnp.zeros((SB_TQ, SB_TK), F32)
    carry = (zeros, zeros, jnp.zeros((SB_TQ, LANES), F32))
    blocks_per_q = SB_TQ // SB_TK
    for d in reversed(range(blocks_per_q)):
        carry = step(i * blocks_per_q + d, carry, d * SB_TK)
    n_before = i * blocks_per_q
    carry = lax.fori_loop(0, n_before, lambda n, cr: step(n_before - 1 - n, cr, None), carry)
    o_ref[...] = carry[2].astype(BF16)


def _stickbreak(qkv, w2):
    nq = SEQ // SB_TQ
    n_pairs = SB_WIDTH // LANES
    return pl.pallas_call(
        _sb_kernel,
        grid=(BATCH, n_pairs, nq),
        in_specs=[
            pl.BlockSpec((SB_TQ, LANES), lambda b, p, i: (b * nq + i, p)),
            pl.BlockSpec((SEQ, LANES), lambda b, p, i: (b, n_pairs + p)),
            pl.BlockSpec((SEQ, LANES), lambda b, p, i: (b, 2 * n_pairs + p)),
            _resident((2 * SB_TK, 2 * SB_TK)),
        ],
        out_specs=pl.BlockSpec((SB_TQ, LANES), lambda b, p, i: (b * nq + i, p)),
        out_shape=jax.ShapeDtypeStruct((TOKENS, SB_WIDTH), BF16),
        compiler_params=_params(("arbitrary", "arbitrary", "arbitrary")),
        name="stickbreak",
    )(qkv, qkv, qkv, w2)


def _ffn_kernel(x_ref, mix_ref, p_ref, wout_ref, fng_ref, wup_ref, cw_ref, cb_ref, wdn_ref,
                png_ref, wgate_ref, wproj_ref, ppg_ref, fin_ref, o_ref,
                x1_s, h_s, y_s, abuf_s, prev_s, *, final_norm):
    tm = ROW_TILE
    i = pl.program_id(0)
    seq_start = (i % (SEQ // tm)) == 0

    x1 = x_ref[...] + jnp.dot(mix_ref[...], wout_ref[...], preferred_element_type=F32)
    x1_s[...] = x1
    h_s[...] = _rms(x1, fng_ref[...]).astype(BF16)
    y_s[...] = jnp.zeros_like(y_s)

    def chunk(c, _):
        a = jnp.dot(h_s[...], wup_ref[c], preferred_element_type=F32)
        abuf_s[:SUBLANES, :] = jnp.where(seq_start, 0.0, prev_s[c])
        abuf_s[SUBLANES:, :] = a
        prev_s[c] = a[tm - SUBLANES:, :]
        cw = cw_ref[c]
        conv = (cb_ref[c] + cw[2:3, :] * a
                + cw[1:2, :] * abuf_s[SUBLANES - 1:SUBLANES - 1 + tm, :]
                + cw[0:1, :] * abuf_s[SUBLANES - 2:SUBLANES - 2 + tm, :])
        act = _gelu(conv[:, :FF_CHUNK]) * conv[:, FF_CHUNK:]
        y_s[...] += jnp.dot(act.astype(BF16), wdn_ref[c], preferred_element_type=F32)
        return 0

    lax.fori_loop(0, N_FF_CHUNKS, chunk, 0)

    x2 = x1_s[...] + y_s[...]
    gate = _sigmoid(jnp.dot(_rms(x2, png_ref[...]).astype(BF16), wgate_ref[...], preferred_element_type=F32))
    pp = _rms(jnp.dot(p_ref[...].astype(BF16), wproj_ref[...], preferred_element_type=F32), ppg_ref[...])
    x3 = x2 + gate * pp
    if final_norm:
        x3 = _rms(x3, fin_ref[...])
    o_ref[...] = x3


def _ffn_block(x, mix, p, wout, fng, wup, cw, cb, wdn, png, wgate, wproj, ppg, fin, *, final_norm):
    tm = ROW_TILE
    kmix = mix.shape[1]
    return pl.pallas_call(
        functools.partial(_ffn_kernel, final_norm=final_norm),
        grid=(TOKENS // tm,),
        in_specs=[
            pl.BlockSpec((tm, D_MODEL), lambda i: (i, 0)),
            pl.BlockSpec((tm, kmix), lambda i: (i, 0)),
            pl.BlockSpec((tm, PLE_DIM), lambda i: (i, 0)),
            _resident((kmix, D_MODEL)),
            _resident((1, D_MODEL)),
            _resident((N_FF_CHUNKS, D_MODEL, 2 * FF_CHUNK)),
            _resident((N_FF_CHUNKS, 3, 2 * FF_CHUNK)),
            _resident((N_FF_CHUNKS, 1, 2 * FF_CHUNK)),
            _resident((N_FF_CHUNKS, FF_CHUNK, D_MODEL)),
            _resident((1, D_MODEL)),
            _resident((D_MODEL, D_MODEL)),
            _resident((PLE_DIM, D_MODEL)),
            _resident((1, D_MODEL)),
            _resident((1, D_MODEL)),
        ],
        out_specs=pl.BlockSpec((tm, D_MODEL), lambda i: (i, 0)),
        out_shape=jax.ShapeDtypeStruct((TOKENS, D_MODEL), F32),
        scratch_shapes=[
            pltpu.VMEM((tm, D_MODEL), F32),
            pltpu.VMEM((tm, D_MODEL), BF16),
            pltpu.VMEM((tm, D_MODEL), F32),
            pltpu.VMEM((tm + SUBLANES, 2 * FF_CHUNK), F32),
            pltpu.VMEM((N_FF_CHUNKS, SUBLANES, 2 * FF_CHUNK), F32),
        ],
        compiler_params=_params(("arbitrary",)),
        name="ffn_block_final" if final_norm else "ffn_block",
    )(x, mix, p, wout, fng, wup, cw, cb, wdn, png, wgate, wproj, ppg, fin)


def _ret_in_kernel(x_ref, g_ref, w_ref, cos_ref, sin_ref, o_ref):
    h = _rms(x_ref[...], g_ref[...]).astype(BF16)
    cos = cos_ref[...]
    sin = sin_ref[...]
    half = RET_DK // 2
    blk = RET_QK
    for jb in range(RET_IN // blk):
        cols = slice(jb * blk, (jb + 1) * blk)
        pr = jnp.dot(h, w_ref[:, cols], preferred_element_type=F32)
        if jb < 2:
            scale = 1.0 if jb == 0 else RET_DK ** -0.5
            for hd in range(RET_HEADS):
                x1 = pr[:, hd * RET_DK:hd * RET_DK + half]
                x2 = pr[:, hd * RET_DK + half:(hd + 1) * RET_DK]
                base = jb * blk + hd * RET_DK
                o_ref[:, base:base + half] = ((x1 * cos - x2 * sin) * scale).astype(BF16)
                o_ref[:, base + half:base + RET_DK] = ((x2 * cos + x1 * sin) * scale).astype(BF16)
        elif jb < 2 + RET_V // blk:
            o_ref[:, cols] = pr.astype(BF16)
        else:
            o_ref[:, cols] = (pr * _sigmoid(pr)).astype(BF16)


def _ret_in(x, g, w, cos, sin):
    tm = ROW_TILE
    tiles_per_seq = SEQ // tm
    return pl.pallas_call(
        _ret_in_kernel,
        grid=(TOKENS // tm,),
        in_specs=[
            pl.BlockSpec((tm, D_MODEL), lambda i: (i, 0)),
            _resident((1, D_MODEL)),
            _resident((D_MODEL, RET_IN)),
            pl.BlockSpec((tm, RET_DK // 2), lambda i: (i % tiles_per_seq, 0)),
            pl.BlockSpec((tm, RET_DK // 2), lambda i: (i % tiles_per_seq, 0)),
        ],
        out_specs=pl.BlockSpec((tm, RET_IN), lambda i: (i, 0)),
        out_shape=jax.ShapeDtypeStruct((TOKENS, RET_IN), BF16),
        compiler_params=_params(("arbitrary",)),
        name="ret_in",
    )(x, g, w, cos, sin)


def _retention_kernel(q_ref, k_ref, v_ref, sg_ref, dec_ref, zeta_ref, xi_ref, cd_ref, gn_ref, o_ref, state_s):
    state_s[...] = jnp.zeros_like(state_s)
    dec = dec_ref[...]
    zeta = zeta_ref[...]
    xi = xi_ref[...]
    cd = cd_ref[...]
    gn = gn_ref[...]

    def chunk(c, _):
        rows = pl.ds(pl.multiple_of(c * CHUNK, CHUNK), CHUNK)
        q = q_ref[rows, :]
        k = k_ref[rows, :]
        v = v_ref[rows, :]
        st = state_s[...]
        inner = lax.dot_general(q, k, (((1,), (1,)), ((), ())), preferred_element_type=F32) * dec
        o = (jnp.dot(inner.astype(BF16), v, preferred_element_type=F32)
             + jnp.dot(q, st.astype(BF16), preferred_element_type=F32) * xi)
        kz = (k.astype(F32) * zeta).astype(BF16)
        state_s[...] = st * cd + lax.dot_general(kz, v, (((0,), (0,)), ((), ())), preferred_element_type=F32)
        d = o - jnp.mean(o, axis=-1, keepdims=True)
        var = jnp.mean(d * d, axis=-1, keepdims=True)
        on = d * lax.rsqrt(var + LN_EPS) * gn
        o_ref[rows, :] = (sg_ref[rows, :].astype(F32) * on).astype(BF16)
        return 0

    lax.fori_loop(0, SEQ // CHUNK, chunk, 0)


def _retention(r, dec, zeta, xi, cd, gn):
    h = RET_HEADS
    return pl.pallas_call(
        _retention_kernel,
        grid=(BATCH, h),
        in_specs=[
            pl.BlockSpec((SEQ, RET_DK), lambda b, hd: (b, hd)),
            pl.BlockSpec((SEQ, RET_DK), lambda b, hd: (b, h + hd)),
            pl.BlockSpec((SEQ, RET_DV), lambda b, hd: (b, h + hd)),
            pl.BlockSpec((SEQ, RET_DV), lambda b, hd: (b, 2 * h + hd)),
            pl.BlockSpec((None, CHUNK, CHUNK), lambda b, hd: (hd, 0, 0)),
            pl.BlockSpec((None, CHUNK, 1), lambda b, hd: (hd, 0, 0)),
            pl.BlockSpec((None, CHUNK, 1), lambda b, hd: (hd, 0, 0)),
            pl.BlockSpec((None, 1, RET_DV), lambda b, hd: (hd, 0, 0)),
            pl.BlockSpec((1, RET_DV), lambda b, hd: (0, hd)),
        ],
        out_specs=pl.BlockSpec((SEQ, RET_DV), lambda b, hd: (b, hd)),
        out_shape=jax.ShapeDtypeStruct((TOKENS, RET_V), BF16),
        scratch_shapes=[pltpu.VMEM((RET_DK, RET_DV), F32)],
        compiler_params=_params(("arbitrary", "arbitrary")),
        name="retention",
    )(r, r, r, r, dec, zeta, xi, cd, gn)


def _cumsum_weights():
    s_from = jnp.arange(SB_TK)[:, None]
    s_to = jnp.arange(SB_TK)[None, :]
    later = (s_from > s_to).astype(BF16)
    half = jnp.concatenate([later, jnp.ones((SB_TK, SB_TK), BF16)], axis=1)
    return jnp.concatenate([half, half], axis=0)


def _rotary_tables():
    half = RET_DK // 2
    inv = 1.0 / (ROPE_BASE ** (jnp.arange(half, dtype=F32) / half))
    ang = jnp.arange(SEQ, dtype=F32)[:, None] * inv[None, :]
    return jnp.cos(ang), jnp.sin(ang)


def _retention_tables():
    log_gamma = jnp.log(1.0 - 2.0 ** (-5.0 - jnp.arange(RET_HEADS, dtype=F32)))
    idx = jnp.arange(CHUNK, dtype=F32)
    diff = idx[:, None] - idx[None, :]
    dec = jnp.where(diff[None] >= 0, jnp.exp(diff[None] * log_gamma[:, None, None]), 0.0)
    zeta = jnp.exp((CHUNK - 1 - idx)[None, :] * log_gamma[:, None])[:, :, None]
    xi = jnp.exp((idx + 1.0)[None, :] * log_gamma[:, None])[:, :, None]
    cd = jnp.broadcast_to(jnp.exp(CHUNK * log_gamma)[:, None, None], (RET_HEADS, 1, RET_DV))
    return dec, zeta, xi, cd


def _ffn_params(w_up, conv_w, conv_b, w_down):
    def pair(t):
        lead = t.shape[:-1]
        t = t.reshape(lead + (2, N_FF_CHUNKS, FF_CHUNK))
        t = jnp.moveaxis(t, -2, 0)
        return t.reshape((N_FF_CHUNKS,) + lead + (2 * FF_CHUNK,))
    return (pair(w_up).astype(BF16), pair(conv_w), pair(conv_b[None, :]),
            w_down.reshape(N_FF_CHUNKS, FF_CHUNK, D_MODEL).astype(BF16))


def kernel(x, p, mix_norm_g, ffn_norm_g, ple_norm_g, ab_w_in, sg_ln_g, sg_ln_b, sg_w, sg_b, ab_w_out,
           ret_w_in, ret_gn_g, ret_w_out, ffn_w_up, ffn_conv_w, ffn_conv_b, ffn_w_down,
           ple_w_gate, ple_w_proj, ple_post_g, final_norm_g):
    xt = x.reshape(TOKENS, D_MODEL)
    pt = p.reshape(2, TOKENS, PLE_DIM)
    row = lambda t: t.reshape(1, -1)

    def tail(layer, xin, mix, wout, final_norm):
        wup, cw, cb, wdn = _ffn_params(ffn_w_up[layer], ffn_conv_w[layer], ffn_conv_b[layer], ffn_w_down[layer])
        return _ffn_block(xin, mix, pt[layer], wout.astype(BF16), row(ffn_norm_g[layer]), wup, cw, cb, wdn,
                          row(ple_norm_g[layer]), ple_w_gate[layer].astype(BF16), ple_w_proj[layer].astype(BF16),
                          row(ple_post_g[layer]), row(final_norm_g), final_norm=final_norm)

    w_in = ab_w_in[0].astype(BF16)
    a_out, qkv = _mix0_in(xt, row(mix_norm_g[0]), w_in[:, :2 * SG_WIDTH], w_in[:, 2 * SG_WIDTH:],
                          row(sg_ln_g[0]), row(sg_ln_b[0]), sg_w[0],
                          jnp.broadcast_to(sg_b[0][:, :, None], (SG_GROUPS, CHUNK, LANES)))
    b_out = _stickbreak(qkv, _cumsum_weights())
    xt = tail(0, xt, jnp.concatenate([a_out, b_out], axis=1), ab_w_out[0], False)

    cos, sin = _rotary_tables()
    r = _ret_in(xt, row(mix_norm_g[1]), ret_w_in[0].astype(BF16), cos, sin)
    gated = _retention(r, *_retention_tables(), row(ret_gn_g[0]))
    xt = tail(1, xt, gated, ret_w_out[0], True)
    return xt.reshape(BATCH, SEQ, D_MODEL)
```

```python
import functools

import jax
import jax.numpy as jnp
from jax import lax
from jax.experimental import pallas as pl
from jax.experimental.pallas import tpu as pltpu

F32 = jnp.float32
BF16 = jnp.bfloat16

D_MODEL = 1024
BATCH = 16
SEQ = 2048
TOKENS = BATCH * SEQ
DEPTH = 2
PLE_DIM = 256
CHUNK = 128
SG_GROUPS = 4
SG_WIDTH = 512
SB_HEADS = 8
SB_HEAD_DIM = 64
SB_WIDTH = 512
AB_IN = 2 * SG_WIDTH + 3 * SB_WIDTH
RET_HEADS = 4
RET_DK = 256
RET_DV = 512
RET_QK = 1024
RET_V = 2048
RET_IN = 6144
ROPE_BASE = 10000.0
D_FF = 2816
CONV_W = 3
RMS_EPS = 1e-6
LN_EPS = 1e-5
LOG2_E = 1.4426950408889634

LANES = 128
SUBLANES = 8
VMEM_LIMIT_BYTES = 56 * 1024 * 1024

ROW_TILE = 512
FF_CHUNK = 256
N_FF_CHUNKS = D_FF // FF_CHUNK
SB_TQ = 512
SB_TK = 128


def _resident(shape):
    nd = len(shape)
    return pl.BlockSpec(shape, lambda *_: (0,) * nd, pipeline_mode=pl.Buffered(1))


def _layer_resident(shape, layer):
    nd = len(shape)
    return pl.BlockSpec((None,) + shape, lambda *_: (layer,) + (0,) * nd, pipeline_mode=pl.Buffered(1))


def _params(semantics):
    return pltpu.CompilerParams(dimension_semantics=semantics, vmem_limit_bytes=VMEM_LIMIT_BYTES)


def _rms(x, g):
    return x * lax.rsqrt(jnp.mean(x * x, axis=-1, keepdims=True) + RMS_EPS) * g


def _gelu(x):
    return 0.5 * x * (1.0 + lax.erf(x * (0.5 ** 0.5)))


def _sigmoid(x):
    return 1.0 / (1.0 + jnp.exp(-x))


def _mix0_in_kernel(x_ref, g_ref, win_ref, lng_ref, lnb_ref, ws_ref, bs_ref, a_ref, qkv_ref):
    h = _rms(x_ref[...], g_ref[...]).astype(BF16)
    uv = _gelu(jnp.dot(h, win_ref[:, :2 * SG_WIDTH], preferred_element_type=F32))
    qkv = jnp.dot(h, win_ref[:, 2 * SG_WIDTH:], preferred_element_type=F32)
    qkv_ref[:, :SB_WIDTH] = (qkv[:, :SB_WIDTH] * (SB_HEAD_DIM ** -0.5 * LOG2_E)).astype(BF16)
    qkv_ref[:, SB_WIDTH:] = qkv[:, SB_WIDTH:].astype(BF16)

    n_chunks = ROW_TILE // CHUNK
    row = lax.broadcasted_iota(jnp.int32, (CHUNK, CHUNK), 0)
    col = lax.broadcasted_iota(jnp.int32, (CHUNK, CHUNK), 1)
    for g in range(SG_GROUPS):
        gs = slice(g * LANES, (g + 1) * LANES)
        w = jnp.where(row >= col, ws_ref[g], 0.0).astype(BF16)
        vn = []
        for c in range(n_chunks):
            vf = uv[c * CHUNK:(c + 1) * CHUNK, SG_WIDTH + g * LANES:SG_WIDTH + (g + 1) * LANES]
            d = vf - jnp.mean(vf, axis=-1, keepdims=True)
            var = jnp.mean(d * d, axis=-1, keepdims=True)
            vn.append((d * lax.rsqrt(var + LN_EPS) * lng_ref[:, gs] + lnb_ref[:, gs]).astype(BF16))
        mixed = jnp.dot(w, jnp.concatenate(vn, axis=1), preferred_element_type=F32)
        for c in range(n_chunks):
            m = mixed[:, c * LANES:(c + 1) * LANES] + bs_ref[g]
            a_ref[c * CHUNK:(c + 1) * CHUNK, gs] = (uv[c * CHUNK:(c + 1) * CHUNK, gs] * m).astype(BF16)


def _mix0_in(x, g, win, lng, lnb, ws, bs):
    tm = ROW_TILE
    return pl.pallas_call(
        _mix0_in_kernel,
        grid=(TOKENS // tm,),
        in_specs=[
            pl.BlockSpec((tm, D_MODEL), lambda i: (i, 0)),
            _resident((1, D_MODEL)),
            _resident((D_MODEL, AB_IN)),
            _resident((1, SG_WIDTH)),
            _resident((1, SG_WIDTH)),
            _resident((SG_GROUPS, CHUNK, CHUNK)),
            _resident((SG_GROUPS, CHUNK, LANES)),
        ],
        out_specs=[
            pl.BlockSpec((tm, SG_WIDTH), lambda i: (i, 0)),
            pl.BlockSpec((tm, 3 * SB_WIDTH), lambda i: (i, 0)),
        ],
        out_shape=[
            jax.ShapeDtypeStruct((TOKENS, SG_WIDTH), BF16),
            jax.ShapeDtypeStruct((TOKENS, 3 * SB_WIDTH), BF16),
        ],
        compiler_params=_params(("arbitrary",)),
        name="mix0_in",
    )(x, g, win, lng, lnb, ws, bs)


def _sb_kernel(q_ref, k_ref, v_ref, w_ref, o_ref):
    i = pl.program_id(2)
    lane = lax.broadcasted_iota(jnp.int32, (1, LANES), 1)
    first_head = lane < SB_HEAD_DIM
    wcs = w_ref[...]

    def split_heads(blk):
        zero = jnp.zeros_like(blk)
        return jnp.concatenate([jnp.where(first_head, blk, zero), jnp.where(first_head, zero, blk)], axis=0)

    def step(j_hi, n_blocks, carry, diag_offset=None, row0=0):
        c_all, acc_all = carry
        c, acc = c_all[row0:], acc_all[row0:]
        w = 2 * SB_TK
        kcat, vcat = [], []
        for u in range(n_blocks):
            r0 = pl.multiple_of((j_hi - u) * SB_TK, SB_TK)
            kcat.append(split_heads(k_ref[pl.ds(r0, SB_TK), :]))
            vcat.append(split_heads(v_ref[pl.ds(r0, SB_TK), :]))
        y = lax.dot_general(q_ref[row0:, :], jnp.concatenate(kcat, axis=0), (((1,), (1,)), ((), ())),
                            preferred_element_type=F32)
        sp = jnp.maximum(y, 0.0) + jnp.log(1.0 + jnp.exp2(-jnp.abs(y))) * LOG2_E
        if diag_offset is not None:
            row = row0 + lax.broadcasted_iota(jnp.int32, y.shape, 0)
            col = lax.broadcasted_iota(jnp.int32, y.shape, 1) & (SB_TK - 1)
            causal = (col + diag_offset) < row
            sp = jnp.where(causal, sp, 0.0)
        sp = sp.astype(BF16)
        cs = [jnp.dot(sp[:, u * w:(u + 1) * w], wcs, preferred_element_type=F32) for u in range(n_blocks)]
        a = []
        for u in range(n_blocks):
            later = c + cs[u]
            a.append(jnp.exp2(y[:, u * w:(u + 1) * w] - later))
            c = jnp.concatenate([jnp.broadcast_to(later[:, hd * SB_TK:hd * SB_TK + 1], (later.shape[0], SB_TK))
                                 for hd in range(2)], axis=1)
        a = jnp.concatenate(a, axis=1)
        if diag_offset is not None:
            a = jnp.where(causal, a, 0.0)
        acc = acc + jnp.dot(a.astype(BF16), jnp.concatenate(vcat, axis=0), preferred_element_type=F32)
        if row0:
            c = jnp.concatenate([c_all[:row0], c], axis=0)
            acc = jnp.concatenate([acc_all[:row0], acc], axis=0)
        return c, acc

    carry = (jnp.zeros((SB_TQ, 2 * SB_TK), F32), jnp.zeros((SB_TQ, LANES), F32))
    blocks_per_q = SB_TQ // SB_TK
    for d in reversed(range(blocks_per_q)):
        carry = step(i * blocks_per_q + d, 1, carry, diag_offset=d * SB_TK, row0=d * SB_TK)
    carry = lax.fori_loop(
        0, i, lambda n, cr: step((i - n) * blocks_per_q - 1, blocks_per_q, cr), carry)
    o_ref[...] = carry[1].astype(BF16)


def _stickbreak(qkv, wcs):
    nq = SEQ // SB_TQ
    n_pairs = SB_WIDTH // LANES
    return pl.pallas_call(
        _sb_kernel,
        grid=(BATCH, n_pairs, nq),
        in_specs=[
            pl.BlockSpec((SB_TQ, LANES), lambda b, p, i: (b * nq + i, p)),
            pl.BlockSpec((SEQ, LANES), lambda b, p, i: (b, n_pairs + p)),
            pl.BlockSpec((SEQ, LANES), lambda b, p, i: (b, 2 * n_pairs + p)),
            _resident((2 * SB_TK, 2 * SB_TK)),
        ],
        out_specs=pl.BlockSpec((SB_TQ, LANES), lambda b, p, i: (b * nq + i, p)),
        out_shape=jax.ShapeDtypeStruct((TOKENS, SB_WIDTH), BF16),
        compiler_params=_params(("arbitrary", "arbitrary", "arbitrary")),
        name="stickbreak",
    )(qkv, qkv, qkv, wcs)


def _ffn_kernel(*refs, n_mix, final_norm):
    x_ref, mix_refs = refs[0], refs[1:1 + n_mix]
    (p_ref, wout_ref, fng_ref, wup_ref, cw_ref, cb_ref, wdn_ref, png_ref, wgate_ref, wproj_ref, ppg_ref,
     fin_ref, o_ref, x1_s, h_s, y_s, prev_s, act_s, a_s) = refs[1 + n_mix:]
    tm = ROW_TILE
    i = pl.program_id(0)
    seq_start = (i % (SEQ // tm)) == 0

    x1 = x_ref[...]
    k0 = 0
    for m_ref in mix_refs:
        x1 = x1 + jnp.dot(m_ref[...], wout_ref[k0:k0 + m_ref.shape[1], :], preferred_element_type=F32)
        k0 += m_ref.shape[1]
    x1_s[...] = x1
    h_s[...] = _rms(x1, fng_ref[...]).astype(BF16)

    n_halves = FF_CHUNK // LANES

    def ff_cols(c, half, part):
        return pl.ds(pl.multiple_of(part * D_FF + c * FF_CHUNK + half * LANES, LANES), LANES)

    def gate_up(ref, c, half):
        return jnp.concatenate([ref[:, ff_cols(c, half, part)] for part in range(2)], axis=1)

    def up(c):
        return [jnp.dot(h_s[...], gate_up(wup_ref, c, half), preferred_element_type=F32) for half in range(n_halves)]

    def conv_act(c):
        act = []
        sub = lax.broadcasted_iota(jnp.int32, (SUBLANES, 2 * LANES), 0)
        for half in range(n_halves):
            a = a_s[half]
            hist = jnp.where(seq_start, 0.0, prev_s[c, half])
            prev_s[c, half] = a[tm - SUBLANES:, :]
            cw = gate_up(cw_ref, c, half)
            conv = gate_up(cb_ref, c, half) + cw[CONV_W - 1:CONV_W, :] * a
            for back in range(1, CONV_W):
                rolled = pltpu.roll(a, back, 0)
                head = jnp.where(sub < back, pltpu.roll(hist, back, 0), rolled[:SUBLANES])
                shifted = jnp.concatenate([head, rolled[SUBLANES:]], axis=0)
                conv = conv + cw[CONV_W - 1 - back:CONV_W - back, :] * shifted
            act.append((_gelu(conv[:, :LANES]) * conv[:, LANES:]).astype(BF16))
        return jnp.concatenate(act, axis=1)

    def down(c):
        rows = pl.ds(pl.multiple_of(c * FF_CHUNK, FF_CHUNK), FF_CHUNK)
        return jnp.dot(act_s[...], wdn_ref[rows, :], preferred_element_type=F32)

    def stage_up(a_halves):
        for half, a in enumerate(a_halves):
            a_s[half] = a

    stage_up(up(0))
    a_next = up(1)
    act_s[...] = conv_act(0)
    stage_up(a_next)
    y_s[...] = jnp.zeros_like(y_s)

    def chunk(c, _):
        a_next = up(c)
        y_s[...] += down(c - 2)
        act_s[...] = conv_act(c - 1)
        stage_up(a_next)
        return 0

    lax.fori_loop(2, N_FF_CHUNKS, chunk, 0)
    y_s[...] += down(N_FF_CHUNKS - 2)
    act_s[...] = conv_act(N_FF_CHUNKS - 1)

    x2 = x1_s[...] + y_s[...] + down(N_FF_CHUNKS - 1)
    gate = _sigmoid(jnp.dot(_rms(x2, png_ref[...]).astype(BF16), wgate_ref[...], preferred_element_type=F32))
    pp = _rms(jnp.dot(p_ref[...].astype(BF16), wproj_ref[...], preferred_element_type=F32), ppg_ref[...])
    x3 = x2 + gate * pp
    if final_norm:
        x3 = _rms(x3, fin_ref[...])
    o_ref[...] = x3


def _ffn_block(layer, x, mixes, p, wout, fng, wup, cw, cb, wdn, png, wgate, wproj, ppg, fin, *, final_norm):
    tm = ROW_TILE
    kmix = sum(m.shape[1] for m in mixes)
    lr = functools.partial(_layer_resident, layer=layer)
    return pl.pallas_call(
        functools.partial(_ffn_kernel, n_mix=len(mixes), final_norm=final_norm),
        grid=(TOKENS // tm,),
        in_specs=[pl.BlockSpec((tm, D_MODEL), lambda i: (i, 0))]
        + [pl.BlockSpec((tm, m.shape[1]), lambda i: (i, 0)) for m in mixes]
        + [
            pl.BlockSpec((None, tm, PLE_DIM), lambda i: (layer, i, 0)),
            _resident((kmix, D_MODEL)),
            lr((1, D_MODEL)),
            lr((D_MODEL, 2 * D_FF)),
            lr((CONV_W, 2 * D_FF)),
            lr((1, 2 * D_FF)),
            lr((D_FF, D_MODEL)),
            lr((1, D_MODEL)),
            lr((D_MODEL, D_MODEL)),
            lr((PLE_DIM, D_MODEL)),
            lr((1, D_MODEL)),
            _resident((1, D_MODEL)),
        ],
        out_specs=pl.BlockSpec((tm, D_MODEL), lambda i: (i, 0)),
        out_shape=jax.ShapeDtypeStruct((TOKENS, D_MODEL), F32),
        scratch_shapes=[
            pltpu.VMEM((tm, D_MODEL), F32),
            pltpu.VMEM((tm, D_MODEL), BF16),
            pltpu.VMEM((tm, D_MODEL), F32),
            pltpu.VMEM((N_FF_CHUNKS, FF_CHUNK // LANES, SUBLANES, 2 * LANES), F32),
            pltpu.VMEM((tm, FF_CHUNK), BF16),
            pltpu.VMEM((FF_CHUNK // LANES, tm, 2 * LANES), F32),
        ],
        compiler_params=_params(("arbitrary",)),
        name="ffn_block_final" if final_norm else "ffn_block",
    )(x, *mixes, p, wout, fng, wup, cw, cb, wdn, png, wgate, wproj, ppg, fin)


def _ret_in_kernel(x_ref, g_ref, w_ref, cos_ref, sin_ref, o_ref):
    h = _rms(x_ref[...], g_ref[...]).astype(BF16)
    cos = cos_ref[...]
    sin = sin_ref[...]
    half = RET_DK // 2
    blk = RET_QK
    for jb in range(RET_IN // blk):
        cols = slice(jb * blk, (jb + 1) * blk)
        pr = jnp.dot(h, w_ref[:, cols], preferred_element_type=F32)
        if jb < 2:
            scale = 1.0 if jb == 0 else RET_DK ** -0.5
            for hd in range(RET_HEADS):
                x1 = pr[:, hd * RET_DK:hd * RET_DK + half]
                x2 = pr[:, hd * RET_DK + half:(hd + 1) * RET_DK]
                base = jb * blk + hd * RET_DK
                o_ref[:, base:base + half] = ((x1 * cos - x2 * sin) * scale).astype(BF16)
                o_ref[:, base + half:base + RET_DK] = ((x2 * cos + x1 * sin) * scale).astype(BF16)
        elif jb < 2 + RET_V // blk:
            o_ref[:, cols] = pr.astype(BF16)
        else:
            o_ref[:, cols] = (pr * _sigmoid(pr)).astype(BF16)


def _ret_in(x, g, w, cos, sin):
    tm = ROW_TILE
    tiles_per_seq = SEQ // tm
    return pl.pallas_call(
        _ret_in_kernel,
        grid=(TOKENS // tm,),
        in_specs=[
            pl.BlockSpec((tm, D_MODEL), lambda i: (i, 0)),
            _layer_resident((1, D_MODEL), 1),
            _resident((D_MODEL, RET_IN)),
            pl.BlockSpec((tm, RET_DK // 2), lambda i: (i % tiles_per_seq, 0)),
            pl.BlockSpec((tm, RET_DK // 2), lambda i: (i % tiles_per_seq, 0)),
        ],
        out_specs=pl.BlockSpec((tm, RET_IN), lambda i: (i, 0)),
        out_shape=jax.ShapeDtypeStruct((TOKENS, RET_IN), BF16),
        compiler_params=_params(("arbitrary",)),
        name="ret_in",
    )(x, g, w, cos, sin)


def _retention_kernel(q_ref, k_ref, v_ref, sg_ref, dec_ref, zeta_ref, xi_ref, cd_ref, gn_ref, o_ref, state_s):
    state_s[...] = jnp.zeros_like(state_s)
    dec = dec_ref[...]
    zeta = zeta_ref[...]
    xi = xi_ref[...]
    cd = cd_ref[...]
    gn = gn_ref[...]

    def chunk(c, _):
        rows = pl.ds(pl.multiple_of(c * CHUNK, CHUNK), CHUNK)
        q = q_ref[rows, :]
        k = k_ref[rows, :]
        v = v_ref[rows, :]
        st = state_s[...]
        inner = lax.dot_general(q, k, (((1,), (1,)), ((), ())), preferred_element_type=F32) * dec
        o = (jnp.dot(inner.astype(BF16), v, preferred_element_type=F32)
             + jnp.dot(q, st.astype(BF16), preferred_element_type=F32) * xi)
        kz = (k.astype(F32) * zeta).astype(BF16)
        state_s[...] = st * cd + lax.dot_general(kz, v, (((0,), (0,)), ((), ())), preferred_element_type=F32)
        d = o - jnp.mean(o, axis=-1, keepdims=True)
        var = jnp.mean(d * d, axis=-1, keepdims=True)
        on = d * lax.rsqrt(var + LN_EPS) * gn
        o_ref[rows, :] = (sg_ref[rows, :].astype(F32) * on).astype(BF16)
        return 0

    lax.fori_loop(0, SEQ // CHUNK, chunk, 0)


def _retention(r, dec, zeta, xi, cd, gn):
    h = RET_HEADS
    return pl.pallas_call(
        _retention_kernel,
        grid=(BATCH, h),
        in_specs=[
            pl.BlockSpec((SEQ, RET_DK), lambda b, hd: (b, hd)),
            pl.BlockSpec((SEQ, RET_DK), lambda b, hd: (b, h + hd)),
            pl.BlockSpec((SEQ, RET_DV), lambda b, hd: (b, h + hd)),
            pl.BlockSpec((SEQ, RET_DV), lambda b, hd: (b, 2 * h + hd)),
            pl.BlockSpec((None, CHUNK, CHUNK), lambda b, hd: (hd, 0, 0)),
            pl.BlockSpec((None, CHUNK, 1), lambda b, hd: (hd, 0, 0)),
            pl.BlockSpec((None, CHUNK, 1), lambda b, hd: (hd, 0, 0)),
            pl.BlockSpec((None, 1, RET_DV), lambda b, hd: (hd, 0, 0)),
            pl.BlockSpec((1, RET_DV), lambda b, hd: (0, hd)),
        ],
        out_specs=pl.BlockSpec((SEQ, RET_DV), lambda b, hd: (b, hd)),
        out_shape=jax.ShapeDtypeStruct((TOKENS, RET_V), BF16),
        scratch_shapes=[pltpu.VMEM((RET_DK, RET_DV), F32)],
        compiler_params=_params(("arbitrary", "arbitrary")),
        name="retention",
    )(r, r, r, r, dec, zeta, xi, cd, gn)


def _later_sum_weights():
    s_from = jnp.arange(2 * SB_TK)[:, None]
    s_to = jnp.arange(2 * SB_TK)[None, :]
    same_head = (s_from // SB_TK) == (s_to // SB_TK)
    return (same_head & (s_from >= s_to)).astype(BF16)


def _rotary_tables():
    half = RET_DK // 2
    inv = 1.0 / (ROPE_BASE ** (jnp.arange(half, dtype=F32) / half))
    ang = jnp.arange(SEQ, dtype=F32)[:, None] * inv[None, :]
    return jnp.cos(ang), jnp.sin(ang)


def _retention_tables():
    log_gamma = jnp.log(1.0 - 2.0 ** (-5.0 - jnp.arange(RET_HEADS, dtype=F32)))
    idx = jnp.arange(CHUNK, dtype=F32)
    diff = idx[:, None] - idx[None, :]
    dec = jnp.where(diff[None] >= 0, jnp.exp(diff[None] * log_gamma[:, None, None]), 0.0)
    zeta = jnp.exp((CHUNK - 1 - idx)[None, :] * log_gamma[:, None])[:, :, None]
    xi = jnp.exp((idx + 1.0)[None, :] * log_gamma[:, None])[:, :, None]
    cd = jnp.broadcast_to(jnp.exp(CHUNK * log_gamma)[:, None, None], (RET_HEADS, 1, RET_DV))
    return dec, zeta, xi, cd


def kernel(x, p, mix_norm_g, ffn_norm_g, ple_norm_g, ab_w_in, sg_ln_g, sg_ln_b, sg_w, sg_b, ab_w_out,
           ret_w_in, ret_gn_g, ret_w_out, ffn_w_up, ffn_conv_w, ffn_conv_b, ffn_w_down,
           ple_w_gate, ple_w_proj, ple_post_g, final_norm_g):
    xt = x.reshape(TOKENS, D_MODEL)
    pt = p.reshape(DEPTH, TOKENS, PLE_DIM)
    rows = lambda t: t.reshape(t.shape[0], 1, -1)
    mix_g, ffn_g, ple_g, post_g = rows(mix_norm_g), rows(ffn_norm_g), rows(ple_norm_g), rows(ple_post_g)
    w_up, w_down = ffn_w_up.astype(BF16), ffn_w_down.astype(BF16)
    w_gate, w_proj = ple_w_gate.astype(BF16), ple_w_proj.astype(BF16)
    conv_b = rows(ffn_conv_b)
    fin = final_norm_g.reshape(1, D_MODEL)

    def tail(layer, xin, mixes, wout, final_norm):
        return _ffn_block(layer, xin, mixes, pt, wout.astype(BF16), ffn_g, w_up, ffn_conv_w, conv_b, w_down,
                          ple_g, w_gate, w_proj, post_g, fin, final_norm=final_norm)

    a_out, qkv = _mix0_in(xt, mix_g[0], ab_w_in[0].astype(BF16), sg_ln_g, sg_ln_b, sg_w[0],
                          jnp.broadcast_to(sg_b[0][:, :, None], (SG_GROUPS, CHUNK, LANES)))
    b_out = _stickbreak(qkv, _later_sum_weights())
    xt = tail(0, xt, (a_out, b_out), ab_w_out[0], False)

    cos, sin = _rotary_tables()
    r = _ret_in(xt, mix_g, ret_w_in[0].astype(BF16), cos, sin)
    gated = _retention(r, *_retention_tables(), ret_gn_g)
    xt = tail(1, xt, (gated,), ret_w_out[0], True)
    return xt.reshape(BATCH, SEQ, D_MODEL)
```

```python
import functools

import jax
import jax.numpy as jnp
from jax import lax
from jax.experimental import pallas as pl
from jax.experimental.pallas import tpu as pltpu

F32 = jnp.float32
BF16 = jnp.bfloat16

D_MODEL = 1024
BATCH = 16
SEQ = 2048
TOKENS = BATCH * SEQ
DEPTH = 2
PLE_DIM = 256
CHUNK = 128
SG_GROUPS = 4
SG_WIDTH = 512
SB_HEADS = 8
SB_HEAD_DIM = 64
SB_WIDTH = 512
AB_IN = 2 * SG_WIDTH + 3 * SB_WIDTH
RET_HEADS = 4
RET_DK = 256
RET_DV = 512
RET_QK = 1024
RET_V = 2048
RET_IN = 6144
ROPE_BASE = 10000.0
D_FF = 2816
CONV_W = 3
RMS_EPS = 1e-6
LN_EPS = 1e-5
LOG2_E = 1.4426950408889634

LANES = 128
SUBLANES = 8
VMEM_LIMIT_BYTES = 56 * 1024 * 1024

ROW_TILE = 512
FF_CHUNK = 256
N_FF_CHUNKS = D_FF // FF_CHUNK
SB_TQ = 512
SB_TK = 128
SB_DEAD = 160.0
RET_CHUNK = 512


def _resident(shape):
    nd = len(shape)
    return pl.BlockSpec(shape, lambda *_: (0,) * nd, pipeline_mode=pl.Buffered(1))


def _layer_resident(shape, layer):
    nd = len(shape)
    return pl.BlockSpec((None,) + shape, lambda *_: (layer,) + (0,) * nd, pipeline_mode=pl.Buffered(1))


def _params(semantics):
    return pltpu.CompilerParams(dimension_semantics=semantics, vmem_limit_bytes=VMEM_LIMIT_BYTES)


def _rms(x, g):
    return x * lax.rsqrt(jnp.mean(x * x, axis=-1, keepdims=True) + RMS_EPS) * g


def _gelu(x):
    return 0.5 * x * (1.0 + lax.erf(x * (0.5 ** 0.5)))


def _sigmoid(x):
    return 1.0 / (1.0 + jnp.exp(-x))


def _mix0_in_kernel(x_ref, g_ref, win_ref, lng_ref, lnb_ref, ws_ref, bs_ref, a_ref, qkv_ref):
    h = _rms(x_ref[...], g_ref[...]).astype(BF16)
    uv = _gelu(jnp.dot(h, win_ref[:, :2 * SG_WIDTH], preferred_element_type=F32))
    qkv = jnp.dot(h, win_ref[:, 2 * SG_WIDTH:], preferred_element_type=F32)
    qkv_ref[:, :SB_WIDTH] = (qkv[:, :SB_WIDTH] * (SB_HEAD_DIM ** -0.5 * LOG2_E)).astype(BF16)
    qkv_ref[:, SB_WIDTH:] = qkv[:, SB_WIDTH:].astype(BF16)

    n_chunks = ROW_TILE // CHUNK
    row = lax.broadcasted_iota(jnp.int32, (CHUNK, CHUNK), 0)
    col = lax.broadcasted_iota(jnp.int32, (CHUNK, CHUNK), 1)
    for g in range(SG_GROUPS):
        gs = slice(g * LANES, (g + 1) * LANES)
        w = jnp.where(row >= col, ws_ref[g], 0.0).astype(BF16)
        vn = []
        for c in range(n_chunks):
            vf = uv[c * CHUNK:(c + 1) * CHUNK, SG_WIDTH + g * LANES:SG_WIDTH + (g + 1) * LANES]
            d = vf - jnp.mean(vf, axis=-1, keepdims=True)
            var = jnp.mean(d * d, axis=-1, keepdims=True)
            vn.append((d * lax.rsqrt(var + LN_EPS) * lng_ref[:, gs] + lnb_ref[:, gs]).astype(BF16))
        mixed = jnp.dot(w, jnp.concatenate(vn, axis=1), preferred_element_type=F32)
        for c in range(n_chunks):
            m = mixed[:, c * LANES:(c + 1) * LANES] + bs_ref[g]
            a_ref[c * CHUNK:(c + 1) * CHUNK, gs] = (uv[c * CHUNK:(c + 1) * CHUNK, gs] * m).astype(BF16)


def _mix0_in(x, g, win, lng, lnb, ws, bs):
    tm = ROW_TILE
    return pl.pallas_call(
        _mix0_in_kernel,
        grid=(TOKENS // tm,),
        in_specs=[
            pl.BlockSpec((tm, D_MODEL), lambda i: (i, 0)),
            _resident((1, D_MODEL)),
            _resident((D_MODEL, AB_IN)),
            _resident((1, SG_WIDTH)),
            _resident((1, SG_WIDTH)),
            _resident((SG_GROUPS, CHUNK, CHUNK)),
            _resident((SG_GROUPS, CHUNK, LANES)),
        ],
        out_specs=[
            pl.BlockSpec((tm, SG_WIDTH), lambda i: (i, 0)),
            pl.BlockSpec((tm, 3 * SB_WIDTH), lambda i: (i, 0)),
        ],
        out_shape=[
            jax.ShapeDtypeStruct((TOKENS, SG_WIDTH), BF16),
            jax.ShapeDtypeStruct((TOKENS, 3 * SB_WIDTH), BF16),
        ],
        compiler_params=_params(("arbitrary",)),
        name="mix0_in",
    )(x, g, win, lng, lnb, ws, bs)


def _sb_kernel(q_ref, k_ref, v_ref, w_ref, o_ref):
    i = pl.program_id(2)
    lane = lax.broadcasted_iota(jnp.int32, (1, LANES), 1)
    first_head = lane < SB_HEAD_DIM
    wcs = w_ref[...]

    def split_heads(blk):
        zero = jnp.zeros_like(blk)
        return jnp.concatenate([jnp.where(first_head, blk, zero), jnp.where(first_head, zero, blk)], axis=0)

    def step(j_hi, n_blocks, carry, diag_offset=None, row0=0):
        c_all, acc_all = carry
        c, acc = c_all[row0:], acc_all[row0:]
        w = 2 * SB_TK
        kcat, vcat = [], []
        for u in range(n_blocks):
            r0 = pl.multiple_of((j_hi - u) * SB_TK, SB_TK)
            kcat.append(split_heads(k_ref[pl.ds(r0, SB_TK), :]))
            vcat.append(split_heads(v_ref[pl.ds(r0, SB_TK), :]))
        y = lax.dot_general(q_ref[row0:, :], jnp.concatenate(kcat, axis=0), (((1,), (1,)), ((), ())),
                            preferred_element_type=F32)
        sp = jnp.maximum(y, 0.0) + jnp.log(1.0 + jnp.exp2(-jnp.abs(y))) * LOG2_E
        if diag_offset is not None:
            row = row0 + lax.broadcasted_iota(jnp.int32, y.shape, 0)
            col = lax.broadcasted_iota(jnp.int32, y.shape, 1) & (SB_TK - 1)
            causal = (col + diag_offset) < row
            sp = jnp.where(causal, sp, 0.0)
        sp = sp.astype(BF16)
        cs = [jnp.dot(sp[:, u * w:(u + 1) * w], wcs, preferred_element_type=F32) for u in range(n_blocks)]
        a = []
        for u in range(n_blocks):
            later = c + cs[u]
            a.append(jnp.exp2(y[:, u * w:(u + 1) * w] - later))
            c = jnp.concatenate([jnp.broadcast_to(later[:, hd * SB_TK:hd * SB_TK + 1], (later.shape[0], SB_TK))
                                 for hd in range(2)], axis=1)
        a = jnp.concatenate(a, axis=1)
        if diag_offset is not None:
            a = jnp.where(causal, a, 0.0)
        acc = acc + jnp.dot(a.astype(BF16), jnp.concatenate(vcat, axis=0), preferred_element_type=F32)
        if row0:
            c = jnp.concatenate([c_all[:row0], c], axis=0)
            acc = jnp.concatenate([acc_all[:row0], acc], axis=0)
        return c, acc

    carry = (jnp.zeros((SB_TQ, 2 * SB_TK), F32), jnp.zeros((SB_TQ, LANES), F32))
    blocks_per_q = SB_TQ // SB_TK
    for d in reversed(range(blocks_per_q)):
        carry = step(i * blocks_per_q + d, 1, carry, diag_offset=d * SB_TK, row0=d * SB_TK)

    def live(c):
        per_row = jnp.minimum(c[:, 0:1], c[:, SB_TK:SB_TK + 1])
        return (jnp.min(per_row, axis=0, keepdims=True)[0, 0] < SB_DEAD).astype(jnp.int32)

    def group(state):
        n, _, c, acc = state
        c, acc = step((i - n) * blocks_per_q - 1, blocks_per_q, (c, acc))
        return n + 1, live(c), c, acc

    state = lax.while_loop(lambda st: (st[0] < i) & (st[1] > 0), group, (0, live(carry[0]), *carry))
    o_ref[...] = state[3].astype(BF16)


def _stickbreak(qkv, wcs):
    nq = SEQ // SB_TQ
    n_pairs = SB_WIDTH // LANES
    return pl.pallas_call(
        _sb_kernel,
        grid=(BATCH, n_pairs, nq),
        in_specs=[
            pl.BlockSpec((SB_TQ, LANES), lambda b, p, i: (b * nq + i, p)),
            pl.BlockSpec((SEQ, LANES), lambda b, p, i: (b, n_pairs + p)),
            pl.BlockSpec((SEQ, LANES), lambda b, p, i: (b, 2 * n_pairs + p)),
            _resident((2 * SB_TK, 2 * SB_TK)),
        ],
        out_specs=pl.BlockSpec((SB_TQ, LANES), lambda b, p, i: (b * nq + i, p)),
        out_shape=jax.ShapeDtypeStruct((TOKENS, SB_WIDTH), BF16),
        compiler_params=_params(("arbitrary", "arbitrary", "arbitrary")),
        name="stickbreak",
    )(qkv, qkv, qkv, wcs)


def _ffn_kernel(*refs, n_mix, final_norm):
    x_ref, mix_refs = refs[0], refs[1:1 + n_mix]
    (p_ref, wout_ref, fng_ref, wup_ref, cw_ref, cb_ref, wdn_ref, png_ref, wgate_ref, wproj_ref, ppg_ref,
     fin_ref, o_ref, x1_s, h_s, y_s, prev_s, act_s, a_s) = refs[1 + n_mix:]
    tm = ROW_TILE
    i = pl.program_id(0)
    seq_start = (i % (SEQ // tm)) == 0

    x1 = x_ref[...]
    k0 = 0
    for m_ref in mix_refs:
        x1 = x1 + jnp.dot(m_ref[...], wout_ref[k0:k0 + m_ref.shape[1], :], preferred_element_type=F32)
        k0 += m_ref.shape[1]
    x1_s[...] = x1
    h_s[...] = _rms(x1, fng_ref[...]).astype(BF16)

    n_halves = FF_CHUNK // LANES

    def ff_cols(c, half, part):
        return pl.ds(pl.multiple_of(part * D_FF + c * FF_CHUNK + half * LANES, LANES), LANES)

    def gate_up(ref, c, half):
        return jnp.concatenate([ref[:, ff_cols(c, half, part)] for part in range(2)], axis=1)

    def up(c):
        return [jnp.dot(h_s[...], gate_up(wup_ref, c, half), preferred_element_type=F32) for half in range(n_halves)]

    def conv_act(c):
        act = []
        sub = lax.broadcasted_iota(jnp.int32, (SUBLANES, 2 * LANES), 0)
        for half in range(n_halves):
            a = a_s[half]
            hist = jnp.where(seq_start, 0.0, prev_s[c, half])
            prev_s[c, half] = a[tm - SUBLANES:, :]
            cw = gate_up(cw_ref, c, half)
            conv = gate_up(cb_ref, c, half) + cw[CONV_W - 1:CONV_W, :] * a
            for back in range(1, CONV_W):
                rolled = pltpu.roll(a, back, 0)
                head = jnp.where(sub < back, pltpu.roll(hist, back, 0), rolled[:SUBLANES])
                shifted = jnp.concatenate([head, rolled[SUBLANES:]], axis=0)
                conv = conv + cw[CONV_W - 1 - back:CONV_W - back, :] * shifted
            act.append((_gelu(conv[:, :LANES]) * conv[:, LANES:]).astype(BF16))
        return jnp.concatenate(act, axis=1)

    def down(c, act):
        rows = pl.ds(pl.multiple_of(c * FF_CHUNK, FF_CHUNK), FF_CHUNK)
        return jnp.dot(act, wdn_ref[rows, :], preferred_element_type=F32)

    def stage_up(a_halves):
        for half, a in enumerate(a_halves):
            a_s[half] = a

    stage_up(up(0))
    y_s[...] = jnp.zeros_like(y_s)

    def chunk(c, _):
        a_next = up(c)
        y_s[...] += down(c - 1, conv_act(c - 1))
        stage_up(a_next)
        return 0

    lax.fori_loop(1, N_FF_CHUNKS, chunk, 0)
    act_s[...] = conv_act(N_FF_CHUNKS - 1)

    x2 = x1_s[...] + y_s[...] + down(N_FF_CHUNKS - 1, act_s[...])
    gate = _sigmoid(jnp.dot(_rms(x2, png_ref[...]).astype(BF16), wgate_ref[...], preferred_element_type=F32))
    pp = _rms(jnp.dot(p_ref[...].astype(BF16), wproj_ref[...], preferred_element_type=F32), ppg_ref[...])
    x3 = x2 + gate * pp
    if final_norm:
        x3 = _rms(x3, fin_ref[...])
    o_ref[...] = x3


def _ffn_block(layer, x, mixes, p, wout, fng, wup, cw, cb, wdn, png, wgate, wproj, ppg, fin, *, final_norm):
    tm = ROW_TILE
    kmix = sum(m.shape[1] for m in mixes)
    lr = functools.partial(_layer_resident, layer=layer)
    return pl.pallas_call(
        functools.partial(_ffn_kernel, n_mix=len(mixes), final_norm=final_norm),
        grid=(TOKENS // tm,),
        in_specs=[pl.BlockSpec((tm, D_MODEL), lambda i: (i, 0))]
        + [pl.BlockSpec((tm, m.shape[1]), lambda i: (i, 0)) for m in mixes]
        + [
            pl.BlockSpec((None, tm, PLE_DIM), lambda i: (layer, i, 0)),
            _resident((kmix, D_MODEL)),
            lr((1, D_MODEL)),
            lr((D_MODEL, 2 * D_FF)),
            lr((CONV_W, 2 * D_FF)),
            lr((1, 2 * D_FF)),
            lr((D_FF, D_MODEL)),
            lr((1, D_MODEL)),
            lr((D_MODEL, D_MODEL)),
            lr((PLE_DIM, D_MODEL)),
            lr((1, D_MODEL)),
            _resident((1, D_MODEL)),
        ],
        out_specs=pl.BlockSpec((tm, D_MODEL), lambda i: (i, 0)),
        out_shape=jax.ShapeDtypeStruct((TOKENS, D_MODEL), F32),
        scratch_shapes=[
            pltpu.VMEM((tm, D_MODEL), F32),
            pltpu.VMEM((tm, D_MODEL), BF16),
            pltpu.VMEM((tm, D_MODEL), F32),
            pltpu.VMEM((N_FF_CHUNKS, FF_CHUNK // LANES, SUBLANES, 2 * LANES), F32),
            pltpu.VMEM((tm, FF_CHUNK), BF16),
            pltpu.VMEM((FF_CHUNK // LANES, tm, 2 * LANES), F32),
        ],
        compiler_params=_params(("arbitrary",)),
        name="ffn_block_final" if final_norm else "ffn_block",
    )(x, *mixes, p, wout, fng, wup, cw, cb, wdn, png, wgate, wproj, ppg, fin)


def _ret_in_kernel(x_ref, g_ref, w_ref, cos_ref, sin_ref, o_ref):
    h = _rms(x_ref[...], g_ref[...]).astype(BF16)
    cos = cos_ref[...]
    sin = sin_ref[...]
    half = RET_DK // 2
    blk = RET_QK
    for jb in range(RET_IN // blk):
        cols = slice(jb * blk, (jb + 1) * blk)
        pr = jnp.dot(h, w_ref[:, cols], preferred_element_type=F32)
        if jb < 2:
            scale = 1.0 if jb == 0 else RET_DK ** -0.5
            for hd in range(RET_HEADS):
                x1 = pr[:, hd * RET_DK:hd * RET_DK + half]
                x2 = pr[:, hd * RET_DK + half:(hd + 1) * RET_DK]
                base = jb * blk + hd * RET_DK
                o_ref[:, base:base + half] = ((x1 * cos - x2 * sin) * scale).astype(BF16)
                o_ref[:, base + half:base + RET_DK] = ((x2 * cos + x1 * sin) * scale).astype(BF16)
        elif jb < 2 + RET_V // blk:
            o_ref[:, cols] = pr.astype(BF16)
        else:
            o_ref[:, cols] = (pr * _sigmoid(pr)).astype(BF16)


def _ret_in(x, g, w, cos, sin):
    tm = ROW_TILE
    tiles_per_seq = SEQ // tm
    return pl.pallas_call(
        _ret_in_kernel,
        grid=(TOKENS // tm,),
        in_specs=[
            pl.BlockSpec((tm, D_MODEL), lambda i: (i, 0)),
            _layer_resident((1, D_MODEL), 1),
            _resident((D_MODEL, RET_IN)),
            pl.BlockSpec((tm, RET_DK // 2), lambda i: (i % tiles_per_seq, 0)),
            pl.BlockSpec((tm, RET_DK // 2), lambda i: (i % tiles_per_seq, 0)),
        ],
        out_specs=pl.BlockSpec((tm, RET_IN), lambda i: (i, 0)),
        out_shape=jax.ShapeDtypeStruct((TOKENS, RET_IN), BF16),
        compiler_params=_params(("arbitrary",)),
        name="ret_in",
    )(x, g, w, cos, sin)


def _retention_kernel(q_ref, k_ref, v_ref, sg_ref, dec_ref, zeta_ref, xi_ref, cd_ref, gn_ref, o_ref, state_s):
    state_s[...] = jnp.zeros_like(state_s)
    dec = dec_ref[...]
    zeta = zeta_ref[...]
    xi = xi_ref[...]
    cd = cd_ref[...]
    gn = gn_ref[...]

    def chunk(c, _):
        rows = pl.ds(pl.multiple_of(c * RET_CHUNK, RET_CHUNK), RET_CHUNK)
        q = q_ref[rows, :]
        k = k_ref[rows, :]
        v = v_ref[rows, :]
        st = state_s[...]
        inner = lax.dot_general(q, k, (((1,), (1,)), ((), ())), preferred_element_type=F32) * dec
        o = (jnp.dot(inner.astype(BF16), v, preferred_element_type=F32)
             + jnp.dot(q, st.astype(BF16), preferred_element_type=F32) * xi)
        kz = (k.astype(F32) * zeta).astype(BF16)
        state_s[...] = st * cd + lax.dot_general(kz, v, (((0,), (0,)), ((), ())), preferred_element_type=F32)
        d = o - jnp.mean(o, axis=-1, keepdims=True)
        var = jnp.mean(d * d, axis=-1, keepdims=True)
        on = d * lax.rsqrt(var + LN_EPS) * gn
        o_ref[rows, :] = (sg_ref[rows, :].astype(F32) * on).astype(BF16)
        return 0

    lax.fori_loop(0, SEQ // RET_CHUNK, chunk, 0)


def _retention(r, dec, zeta, xi, cd, gn):
    h = RET_HEADS
    return pl.pallas_call(
        _retention_kernel,
        grid=(BATCH, h),
        in_specs=[
            pl.BlockSpec((SEQ, RET_DK), lambda b, hd: (b, hd)),
            pl.BlockSpec((SEQ, RET_DK), lambda b, hd: (b, h + hd)),
            pl.BlockSpec((SEQ, RET_DV), lambda b, hd: (b, h + hd)),
            pl.BlockSpec((SEQ, RET_DV), lambda b, hd: (b, 2 * h + hd)),
            pl.BlockSpec((None, RET_CHUNK, RET_CHUNK), lambda b, hd: (hd, 0, 0)),
            pl.BlockSpec((None, RET_CHUNK, 1), lambda b, hd: (hd, 0, 0)),
            pl.BlockSpec((None, RET_CHUNK, 1), lambda b, hd: (hd, 0, 0)),
            pl.BlockSpec((None, 1, RET_DV), lambda b, hd: (hd, 0, 0)),
            pl.BlockSpec((1, RET_DV), lambda b, hd: (0, hd)),
        ],
        out_specs=pl.BlockSpec((SEQ, RET_DV), lambda b, hd: (b, hd)),
        out_shape=jax.ShapeDtypeStruct((TOKENS, RET_V), BF16),
        scratch_shapes=[pltpu.VMEM((RET_DK, RET_DV), F32)],
        compiler_params=_params(("arbitrary", "arbitrary")),
        name="retention",
    )(r, r, r, r, dec, zeta, xi, cd, gn)


def _later_sum_weights():
    s_from = jnp.arange(2 * SB_TK)[:, None]
    s_to = jnp.arange(2 * SB_TK)[None, :]
    same_head = (s_from // SB_TK) == (s_to // SB_TK)
    return (same_head & (s_from >= s_to)).astype(BF16)


def _rotary_tables():
    half = RET_DK // 2
    inv = 1.0 / (ROPE_BASE ** (jnp.arange(half, dtype=F32) / half))
    ang = jnp.arange(SEQ, dtype=F32)[:, None] * inv[None, :]
    return jnp.cos(ang), jnp.sin(ang)


def _retention_tables():
    log_gamma = jnp.log(1.0 - 2.0 ** (-5.0 - jnp.arange(RET_HEADS, dtype=F32)))
    idx = jnp.arange(RET_CHUNK, dtype=F32)
    diff = idx[:, None] - idx[None, :]
    dec = jnp.where(diff[None] >= 0, jnp.exp(diff[None] * log_gamma[:, None, None]), 0.0)
    zeta = jnp.exp((RET_CHUNK - 1 - idx)[None, :] * log_gamma[:, None])[:, :, None]
    xi = jnp.exp((idx + 1.0)[None, :] * log_gamma[:, None])[:, :, None]
    cd = jnp.broadcast_to(jnp.exp(RET_CHUNK * log_gamma)[:, None, None], (RET_HEADS, 1, RET_DV))
    return dec, zeta, xi, cd


def kernel(x, p, mix_norm_g, ffn_norm_g, ple_norm_g, ab_w_in, sg_ln_g, sg_ln_b, sg_w, sg_b, ab_w_out,
           ret_w_in, ret_gn_g, ret_w_out, ffn_w_up, ffn_conv_w, ffn_conv_b, ffn_w_down,
           ple_w_gate, ple_w_proj, ple_post_g, final_norm_g):
    xt = x.reshape(TOKENS, D_MODEL)
    pt = p.reshape(DEPTH, TOKENS, PLE_DIM)
    rows = lambda t: t.reshape(t.shape[0], 1, -1)
    mix_g, ffn_g, ple_g, post_g = rows(mix_norm_g), rows(ffn_norm_g), rows(ple_norm_g), rows(ple_post_g)
    w_up, w_down = ffn_w_up.astype(BF16), ffn_w_down.astype(BF16)
    w_gate, w_proj = ple_w_gate.astype(BF16), ple_w_proj.astype(BF16)
    conv_b = rows(ffn_conv_b)
    fin = final_norm_g.reshape(1, D_MODEL)

    def tail(layer, xin, mixes, wout, final_norm):
        return _ffn_block(layer, xin, mixes, pt, wout.astype(BF16), ffn_g, w_up, ffn_conv_w, conv_b, w_down,
                          ple_g, w_gate, w_proj, post_g, fin, final_norm=final_norm)

    a_out, qkv = _mix0_in(xt, mix_g[0], ab_w_in[0].astype(BF16), sg_ln_g, sg_ln_b, sg_w[0],
                          jnp.broadcast_to(sg_b[0][:, :, None], (SG_GROUPS, CHUNK, LANES)))
    b_out = _stickbreak(qkv, _later_sum_weights())
    xt = tail(0, xt, (a_out, b_out), ab_w_out[0], False)

    cos, sin = _rotary_tables()
    r = _ret_in(xt, mix_g, ret_w_in[0].astype(BF16), cos, sin)
    gated = _retention(r, *_retention_tables(), ret_gn_g)
    xt = tail(1, xt, (gated,), ret_w_out[0], True)
    return xt.reshape(BATCH, SEQ, D_MODEL)
```

```python
import functools

import jax
import jax.numpy as jnp
from jax import lax
from jax.experimental import pallas as pl
from jax.experimental.pallas import tpu as pltpu

F32 = jnp.float32
BF16 = jnp.bfloat16

D_MODEL = 1024
BATCH = 16
SEQ = 2048
TOKENS = BATCH * SEQ
DEPTH = 2
PLE_DIM = 256
CHUNK = 128
SG_GROUPS = 4
SG_WIDTH = 512
SB_HEADS = 8
SB_HEAD_DIM = 64
SB_WIDTH = 512
AB_IN = 2 * SG_WIDTH + 3 * SB_WIDTH
RET_HEADS = 4
RET_DK = 256
RET_DV = 512
RET_QK = 1024
RET_V = 2048
RET_IN = 6144
ROPE_BASE = 10000.0
D_FF = 2816
CONV_W = 3
RMS_EPS = 1e-6
LN_EPS = 1e-5
LOG2_E = 1.4426950408889634

LANES = 128
SUBLANES = 8
VMEM_LIMIT_BYTES = 56 * 1024 * 1024

ROW_TILE = 512
FF_CHUNK = 1408
N_FF_CHUNKS = D_FF // FF_CHUNK
PERM_PITCH = ROW_TILE // SUBLANES + SUBLANES
SB_TQ = 512
SB_TK = 128
SB_DEAD = 160.0
RET_CHUNK = 512


def _resident(shape):
    nd = len(shape)
    return pl.BlockSpec(shape, lambda *_: (0,) * nd, pipeline_mode=pl.Buffered(1))


def _layer_resident(shape, layer):
    nd = len(shape)
    return pl.BlockSpec((None,) + shape, lambda *_: (layer,) + (0,) * nd, pipeline_mode=pl.Buffered(1))


def _params(semantics):
    return pltpu.CompilerParams(dimension_semantics=semantics, vmem_limit_bytes=VMEM_LIMIT_BYTES)


def _rms(x, g):
    return x * lax.rsqrt(jnp.mean(x * x, axis=-1, keepdims=True) + RMS_EPS) * g


def _gelu(x):
    return 0.5 * x * (1.0 + lax.erf(x * (0.5 ** 0.5)))


def _sigmoid(x):
    return 1.0 / (1.0 + jnp.exp(-x))


def _mix0_in_kernel(x_ref, g_ref, win_ref, lng_ref, lnb_ref, ws_ref, bs_ref, a_ref, qkv_ref):
    h = _rms(x_ref[...], g_ref[...]).astype(BF16)
    uv = _gelu(jnp.dot(h, win_ref[:, :2 * SG_WIDTH], preferred_element_type=F32))
    qkv = jnp.dot(h, win_ref[:, 2 * SG_WIDTH:], preferred_element_type=F32)
    qkv_ref[:, :SB_WIDTH] = (qkv[:, :SB_WIDTH] * (SB_HEAD_DIM ** -0.5 * LOG2_E)).astype(BF16)
    qkv_ref[:, SB_WIDTH:] = qkv[:, SB_WIDTH:].astype(BF16)

    n_chunks = ROW_TILE // CHUNK
    row = lax.broadcasted_iota(jnp.int32, (CHUNK, CHUNK), 0)
    col = lax.broadcasted_iota(jnp.int32, (CHUNK, CHUNK), 1)
    for g in range(SG_GROUPS):
        gs = slice(g * LANES, (g + 1) * LANES)
        w = jnp.where(row >= col, ws_ref[g], 0.0).astype(BF16)
        vn = []
        for c in range(n_chunks):
            vf = uv[c * CHUNK:(c + 1) * CHUNK, SG_WIDTH + g * LANES:SG_WIDTH + (g + 1) * LANES]
            d = vf - jnp.mean(vf, axis=-1, keepdims=True)
            var = jnp.mean(d * d, axis=-1, keepdims=True)
            vn.append((d * lax.rsqrt(var + LN_EPS) * lng_ref[:, gs] + lnb_ref[:, gs]).astype(BF16))
        mixed = jnp.dot(w, jnp.concatenate(vn, axis=1), preferred_element_type=F32)
        for c in range(n_chunks):
            m = mixed[:, c * LANES:(c + 1) * LANES] + bs_ref[g]
            a_ref[c * CHUNK:(c + 1) * CHUNK, gs] = (uv[c * CHUNK:(c + 1) * CHUNK, gs] * m).astype(BF16)


def _mix0_in(x, g, win, lng, lnb, ws, bs):
    tm = ROW_TILE
    return pl.pallas_call(
        _mix0_in_kernel,
        grid=(TOKENS // tm,),
        in_specs=[
            pl.BlockSpec((tm, D_MODEL), lambda i: (i, 0)),
            _resident((1, D_MODEL)),
            _resident((D_MODEL, AB_IN)),
            _resident((1, SG_WIDTH)),
            _resident((1, SG_WIDTH)),
            _resident((SG_GROUPS, CHUNK, CHUNK)),
            _resident((SG_GROUPS, CHUNK, LANES)),
        ],
        out_specs=[
            pl.BlockSpec((tm, SG_WIDTH), lambda i: (i, 0)),
            pl.BlockSpec((tm, 3 * SB_WIDTH), lambda i: (i, 0)),
        ],
        out_shape=[
            jax.ShapeDtypeStruct((TOKENS, SG_WIDTH), BF16),
            jax.ShapeDtypeStruct((TOKENS, 3 * SB_WIDTH), BF16),
        ],
        compiler_params=_params(("arbitrary",)),
        name="mix0_in",
    )(x, g, win, lng, lnb, ws, bs)


def _sb_kernel(q_ref, k_ref, v_ref, w_ref, o_ref):
    i = pl.program_id(2)
    lane = lax.broadcasted_iota(jnp.int32, (1, LANES), 1)
    first_head = lane < SB_HEAD_DIM
    wcs = w_ref[...]

    def split_heads(blk):
        zero = jnp.zeros_like(blk)
        return jnp.concatenate([jnp.where(first_head, blk, zero), jnp.where(first_head, zero, blk)], axis=0)

    def step(j_hi, n_blocks, carry, diag_offset=None, row0=0):
        c_all, acc_all = carry
        c, acc = c_all[row0:], acc_all[row0:]
        w = 2 * SB_TK
        kcat, vcat = [], []
        for u in range(n_blocks):
            r0 = pl.multiple_of((j_hi - u) * SB_TK, SB_TK)
            kcat.append(split_heads(k_ref[pl.ds(r0, SB_TK), :]))
            vcat.append(split_heads(v_ref[pl.ds(r0, SB_TK), :]))
        y = lax.dot_general(q_ref[row0:, :], jnp.concatenate(kcat, axis=0), (((1,), (1,)), ((), ())),
                            preferred_element_type=F32)
        sp = jnp.maximum(y, 0.0) + jnp.log(1.0 + jnp.exp2(-jnp.abs(y))) * LOG2_E
        if diag_offset is not None:
            row = row0 + lax.broadcasted_iota(jnp.int32, y.shape, 0)
            col = lax.broadcasted_iota(jnp.int32, y.shape, 1) & (SB_TK - 1)
            causal = (col + diag_offset) < row
            sp = jnp.where(causal, sp, 0.0)
        sp = sp.astype(BF16)
        cs = [jnp.dot(sp[:, u * w:(u + 1) * w], wcs, preferred_element_type=F32) for u in range(n_blocks)]
        a = []
        for u in range(n_blocks):
            later = c + cs[u]
            a.append(jnp.exp2(y[:, u * w:(u + 1) * w] - later))
            c = jnp.concatenate([jnp.broadcast_to(later[:, hd * SB_TK:hd * SB_TK + 1], (later.shape[0], SB_TK))
                                 for hd in range(2)], axis=1)
        a = jnp.concatenate(a, axis=1)
        if diag_offset is not None:
            a = jnp.where(causal, a, 0.0)
        acc = acc + jnp.dot(a.astype(BF16), jnp.concatenate(vcat, axis=0), preferred_element_type=F32)
        if row0:
            c = jnp.concatenate([c_all[:row0], c], axis=0)
            acc = jnp.concatenate([acc_all[:row0], acc], axis=0)
        return c, acc

    carry = (jnp.zeros((SB_TQ, 2 * SB_TK), F32), jnp.zeros((SB_TQ, LANES), F32))
    blocks_per_q = SB_TQ // SB_TK
    for d in reversed(range(blocks_per_q)):
        carry = step(i * blocks_per_q + d, 1, carry, diag_offset=d * SB_TK, row0=d * SB_TK)

    def live(c):
        per_row = jnp.minimum(c[:, 0:1], c[:, SB_TK:SB_TK + 1])
        return (jnp.min(per_row, axis=0, keepdims=True)[0, 0] < SB_DEAD).astype(jnp.int32)

    def group(state):
        n, _, c, acc = state
        c, acc = step((i - n) * blocks_per_q - 1, blocks_per_q, (c, acc))
        return n + 1, live(c), c, acc

    state = lax.while_loop(lambda st: (st[0] < i) & (st[1] > 0), group, (0, live(carry[0]), *carry))
    o_ref[...] = state[3].astype(BF16)


def _stickbreak(qkv, wcs):
    nq = SEQ // SB_TQ
    n_pairs = SB_WIDTH // LANES
    return pl.pallas_call(
        _sb_kernel,
        grid=(BATCH, n_pairs, nq),
        in_specs=[
            pl.BlockSpec((SB_TQ, LANES), lambda b, p, i: (b * nq + i, p)),
            pl.BlockSpec((SEQ, LANES), lambda b, p, i: (b, n_pairs + p)),
            pl.BlockSpec((SEQ, LANES), lambda b, p, i: (b, 2 * n_pairs + p)),
            _resident((2 * SB_TK, 2 * SB_TK)),
        ],
        out_specs=pl.BlockSpec((SB_TQ, LANES), lambda b, p, i: (b * nq + i, p)),
        out_shape=jax.ShapeDtypeStruct((TOKENS, SB_WIDTH), BF16),
        compiler_params=_params(("arbitrary", "arbitrary", "arbitrary")),
        name="stickbreak",
    )(qkv, qkv, qkv, wcs)


def _ffn_kernel(*refs, n_mix, final_norm):
    x_ref, mix_refs = refs[0], refs[1:1 + n_mix]
    (p_ref, wout_ref, fng_ref, wup_ref, cw_ref, cb_ref, wdn_ref, png_ref, wgate_ref, wproj_ref, ppg_ref,
     fin_ref, o_ref, perm_s, xi_s, h_s, prev_s, act_s) = refs[1 + n_mix:]
    tm = ROW_TILE
    n_blocks = tm // SUBLANES
    i = pl.program_id(0)
    seq_start = (i % (SEQ // tm)) == 0

    def interleaved(val):
        n_cb = val.shape[1] // LANES
        for cb in range(n_cb):
            for s in range(SUBLANES):
                perm_s[cb, s * PERM_PITCH:s * PERM_PITCH + n_blocks, :] = (
                    val[s * n_blocks:(s + 1) * n_blocks, cb * LANES:(cb + 1) * LANES])
        return jnp.concatenate(
            [jnp.concatenate([perm_s[cb, pl.ds(j, SUBLANES, stride=PERM_PITCH), :] for cb in range(n_cb)], axis=1)
             for j in range(n_blocks)], axis=0)

    def store_in_position_order(val):
        n_cb = val.shape[1] // LANES
        for j in range(n_blocks):
            for cb in range(n_cb):
                perm_s[cb, pl.ds(j, SUBLANES, stride=PERM_PITCH), :] = (
                    val[j * SUBLANES:(j + 1) * SUBLANES, cb * LANES:(cb + 1) * LANES])
        for cb in range(n_cb):
            for s in range(SUBLANES):
                o_ref[s * n_blocks:(s + 1) * n_blocks, cb * LANES:(cb + 1) * LANES] = (
                    perm_s[cb, s * PERM_PITCH:s * PERM_PITCH + n_blocks, :])

    x1 = x_ref[...]
    k0 = 0
    for m_ref in mix_refs:
        x1 = x1 + jnp.dot(m_ref[...], wout_ref[k0:k0 + m_ref.shape[1], :], preferred_element_type=F32)
        k0 += m_ref.shape[1]
    x1 = interleaved(x1)
    xi_s[...] = x1
    h_s[...] = _rms(x1, fng_ref[...]).astype(BF16)

    n_halves = FF_CHUNK // LANES

    def ff_cols(c, half, part):
        return pl.ds(pl.multiple_of(part * D_FF + c * FF_CHUNK + half * LANES, LANES), LANES)

    def gate_up(ref, c, half):
        return jnp.concatenate([ref[:, ff_cols(c, half, part)] for part in range(2)], axis=1)

    def chunk(c, _):
        sub = lax.broadcasted_iota(jnp.int32, (SUBLANES, 2 * LANES), 0)
        for half in range(n_halves):
            a = jnp.dot(h_s[...], gate_up(wup_ref, c, half), preferred_element_type=F32)
            wrapped = []
            for back in range(1, CONV_W):
                last = a[tm - back * SUBLANES:tm - (back - 1) * SUBLANES, :]
                before = jnp.where(seq_start, 0.0, prev_s[c, half, back - 1])
                prev_s[c, half, back - 1] = last
                wrapped.append(jnp.where(sub == 0, pltpu.roll(before, 1, 0), pltpu.roll(last, 1, 0)))
            cw = gate_up(cw_ref, c, half)
            conv = gate_up(cb_ref, c, half) + cw[CONV_W - 1:CONV_W, :] * a
            for back in range(1, CONV_W):
                shifted = jnp.concatenate(wrapped[back - 1::-1] + [a[:tm - back * SUBLANES, :]], axis=0)
                conv = conv + cw[CONV_W - 1 - back:CONV_W - back, :] * shifted
            gate, upv = conv[:, :LANES], conv[:, LANES:]
            cols = pl.ds(pl.multiple_of(c * FF_CHUNK + half * LANES, LANES), LANES)
            act_s[:, cols] = (gate * (1.0 + lax.erf(gate * (0.5 ** 0.5))) * upv).astype(BF16)
        return 0

    lax.fori_loop(0, N_FF_CHUNKS, chunk, 0)

    x2 = xi_s[...] + 0.5 * jnp.dot(act_s[...], wdn_ref[...], preferred_element_type=F32)
    gate = _sigmoid(jnp.dot(_rms(x2, png_ref[...]).astype(BF16), wgate_ref[...], preferred_element_type=F32))
    pp = _rms(jnp.dot(interleaved(p_ref[...]).astype(BF16), wproj_ref[...], preferred_element_type=F32),
              ppg_ref[...])
    x3 = x2 + gate * pp
    if final_norm:
        x3 = _rms(x3, fin_ref[...])
    store_in_position_order(x3)


def _ffn_block(layer, x, mixes, p, wout, fng, wup, cw, cb, wdn, png, wgate, wproj, ppg, fin, *, final_norm):
    tm = ROW_TILE
    kmix = sum(m.shape[1] for m in mixes)
    lr = functools.partial(_layer_resident, layer=layer)
    return pl.pallas_call(
        functools.partial(_ffn_kernel, n_mix=len(mixes), final_norm=final_norm),
        grid=(TOKENS // tm,),
        in_specs=[pl.BlockSpec((tm, D_MODEL), lambda i: (i, 0))]
        + [pl.BlockSpec((tm, m.shape[1]), lambda i: (i, 0)) for m in mixes]
        + [
            pl.BlockSpec((None, tm, PLE_DIM), lambda i: (layer, i, 0)),
            _resident((kmix, D_MODEL)),
            lr((1, D_MODEL)),
            lr((D_MODEL, 2 * D_FF)),
            lr((CONV_W, 2 * D_FF)),
            lr((1, 2 * D_FF)),
            lr((D_FF, D_MODEL)),
            lr((1, D_MODEL)),
            lr((D_MODEL, D_MODEL)),
            lr((PLE_DIM, D_MODEL)),
            lr((1, D_MODEL)),
            _resident((1, D_MODEL)),
        ],
        out_specs=pl.BlockSpec((tm, D_MODEL), lambda i: (i, 0)),
        out_shape=jax.ShapeDtypeStruct((TOKENS, D_MODEL), F32),
        scratch_shapes=[
            pltpu.VMEM((D_MODEL // LANES, SUBLANES * PERM_PITCH, LANES), F32),
            pltpu.VMEM((tm, D_MODEL), F32),
            pltpu.VMEM((tm, D_MODEL), BF16),
            pltpu.VMEM((N_FF_CHUNKS, FF_CHUNK // LANES, CONV_W - 1, SUBLANES, 2 * LANES), F32),
            pltpu.VMEM((tm, D_FF), BF16),
        ],
        compiler_params=_params(("arbitrary",)),
        name="ffn_block_final" if final_norm else "ffn_block",
    )(x, *mixes, p, wout, fng, wup, cw, cb, wdn, png, wgate, wproj, ppg, fin)


def _ret_in_kernel(x_ref, g_ref, w_ref, cos_ref, sin_ref, o_ref):
    h = _rms(x_ref[...], g_ref[...]).astype(BF16)
    cos = cos_ref[...]
    sin = sin_ref[...]
    half = RET_DK // 2
    blk = RET_QK
    for jb in range(RET_IN // blk):
        cols = slice(jb * blk, (jb + 1) * blk)
        pr = jnp.dot(h, w_ref[:, cols], preferred_element_type=F32)
        if jb < 2:
            scale = 1.0 if jb == 0 else RET_DK ** -0.5
            for hd in range(RET_HEADS):
                x1 = pr[:, hd * RET_DK:hd * RET_DK + half]
                x2 = pr[:, hd * RET_DK + half:(hd + 1) * RET_DK]
                base = jb * blk + hd * RET_DK
                o_ref[:, base:base + half] = ((x1 * cos - x2 * sin) * scale).astype(BF16)
                o_ref[:, base + half:base + RET_DK] = ((x2 * cos + x1 * sin) * scale).astype(BF16)
        elif jb < 2 + RET_V // blk:
            o_ref[:, cols] = pr.astype(BF16)
        else:
            o_ref[:, cols] = (pr * _sigmoid(pr)).astype(BF16)


def _ret_in(x, g, w, cos, sin):
    tm = ROW_TILE
    tiles_per_seq = SEQ // tm
    return pl.pallas_call(
        _ret_in_kernel,
        grid=(TOKENS // tm,),
        in_specs=[
            pl.BlockSpec((tm, D_MODEL), lambda i: (i, 0)),
            _layer_resident((1, D_MODEL), 1),
            _resident((D_MODEL, RET_IN)),
            pl.BlockSpec((tm, RET_DK // 2), lambda i: (i % tiles_per_seq, 0)),
            pl.BlockSpec((tm, RET_DK // 2), lambda i: (i % tiles_per_seq, 0)),
        ],
        out_specs=pl.BlockSpec((tm, RET_IN), lambda i: (i, 0)),
        out_shape=jax.ShapeDtypeStruct((TOKENS, RET_IN), BF16),
        compiler_params=_params(("arbitrary",)),
        name="ret_in",
    )(x, g, w, cos, sin)


def _retention_kernel(q_ref, k_ref, v_ref, sg_ref, dec_ref, zeta_ref, xi_ref, cd_ref, gn_ref, o_ref, state_s):
    state_s[...] = jnp.zeros_like(state_s)
    dec = dec_ref[...]
    zeta = zeta_ref[...]
    xi = xi_ref[...]
    cd = cd_ref[...]
    gn = gn_ref[...]

    def chunk(c, _):
        rows = pl.ds(pl.multiple_of(c * RET_CHUNK, RET_CHUNK), RET_CHUNK)
        q = q_ref[rows, :]
        k = k_ref[rows, :]
        v = v_ref[rows, :]
        st = state_s[...]
        inner = lax.dot_general(q, k, (((1,), (1,)), ((), ())), preferred_element_type=F32) * dec
        o = (jnp.dot(inner.astype(BF16), v, preferred_element_type=F32)
             + jnp.dot(q, st.astype(BF16), preferred_element_type=F32) * xi)
        kz = (k.astype(F32) * zeta).astype(BF16)
        state_s[...] = st * cd + lax.dot_general(kz, v, (((0,), (0,)), ((), ())), preferred_element_type=F32)
        d = o - jnp.mean(o, axis=-1, keepdims=True)
        var = jnp.mean(d * d, axis=-1, keepdims=True)
        on = d * lax.rsqrt(var + LN_EPS) * gn
        o_ref[rows, :] = (sg_ref[rows, :].astype(F32) * on).astype(BF16)
        return 0

    lax.fori_loop(0, SEQ // RET_CHUNK, chunk, 0)


def _retention(r, dec, zeta, xi, cd, gn):
    h = RET_HEADS
    return pl.pallas_call(
        _retention_kernel,
        grid=(BATCH, h),
        in_specs=[
            pl.BlockSpec((SEQ, RET_DK), lambda b, hd: (b, hd)),
            pl.BlockSpec((SEQ, RET_DK), lambda b, hd: (b, h + hd)),
            pl.BlockSpec((SEQ, RET_DV), lambda b, hd: (b, h + hd)),
            pl.BlockSpec((SEQ, RET_DV), lambda b, hd: (b, 2 * h + hd)),
            pl.BlockSpec((None, RET_CHUNK, RET_CHUNK), lambda b, hd: (hd, 0, 0)),
            pl.BlockSpec((None, RET_CHUNK, 1), lambda b, hd: (hd, 0, 0)),
            pl.BlockSpec((None, RET_CHUNK, 1), lambda b, hd: (hd, 0, 0)),
            pl.BlockSpec((None, 1, RET_DV), lambda b, hd: (hd, 0, 0)),
            pl.BlockSpec((1, RET_DV), lambda b, hd: (0, hd)),
        ],
        out_specs=pl.BlockSpec((SEQ, RET_DV), lambda b, hd: (b, hd)),
        out_shape=jax.ShapeDtypeStruct((TOKENS, RET_V), BF16),
        scratch_shapes=[pltpu.VMEM((RET_DK, RET_DV), F32)],
        compiler_params=_params(("arbitrary", "arbitrary")),
        name="retention",
    )(r, r, r, r, dec, zeta, xi, cd, gn)


def _later_sum_weights():
    s_from = jnp.arange(2 * SB_TK)[:, None]
    s_to = jnp.arange(2 * SB_TK)[None, :]
    same_head = (s_from // SB_TK) == (s_to // SB_TK)
    return (same_head & (s_from >= s_to)).astype(BF16)


def _rotary_tables():
    half = RET_DK // 2
    inv = 1.0 / (ROPE_BASE ** (jnp.arange(half, dtype=F32) / half))
    ang = jnp.arange(SEQ, dtype=F32)[:, None] * inv[None, :]
    return jnp.cos(ang), jnp.sin(ang)


def _retention_tables():
    log_gamma = jnp.log(1.0 - 2.0 ** (-5.0 - jnp.arange(RET_HEADS, dtype=F32)))
    idx = jnp.arange(RET_CHUNK, dtype=F32)
    diff = idx[:, None] - idx[None, :]
    dec = jnp.where(diff[None] >= 0, jnp.exp(diff[None] * log_gamma[:, None, None]), 0.0)
    zeta = jnp.exp((RET_CHUNK - 1 - idx)[None, :] * log_gamma[:, None])[:, :, None]
    xi = jnp.exp((idx + 1.0)[None, :] * log_gamma[:, None])[:, :, None]
    cd = jnp.broadcast_to(jnp.exp(RET_CHUNK * log_gamma)[:, None, None], (RET_HEADS, 1, RET_DV))
    return dec, zeta, xi, cd


def kernel(x, p, mix_norm_g, ffn_norm_g, ple_norm_g, ab_w_in, sg_ln_g, sg_ln_b, sg_w, sg_b, ab_w_out,
           ret_w_in, ret_gn_g, ret_w_out, ffn_w_up, ffn_conv_w, ffn_conv_b, ffn_w_down,
           ple_w_gate, ple_w_proj, ple_post_g, final_norm_g):
    xt = x.reshape(TOKENS, D_MODEL)
    pt = p.reshape(DEPTH, TOKENS, PLE_DIM)
    rows = lambda t: t.reshape(t.shape[0], 1, -1)
    mix_g, ffn_g, ple_g, post_g = rows(mix_norm_g), rows(ffn_norm_g), rows(ple_norm_g), rows(ple_post_g)
    w_up, w_down = ffn_w_up.astype(BF16), ffn_w_down.astype(BF16)
    w_gate, w_proj = ple_w_gate.astype(BF16), ple_w_proj.astype(BF16)
    conv_b = rows(ffn_conv_b)
    fin = final_norm_g.reshape(1, D_MODEL)

    def tail(layer, xin, mixes, wout, final_norm):
        return _ffn_block(layer, xin, mixes, pt, wout.astype(BF16), ffn_g, w_up, ffn_conv_w, conv_b, w_down,
                          ple_g, w_gate, w_proj, post_g, fin, final_norm=final_norm)

    a_out, qkv = _mix0_in(xt, mix_g[0], ab_w_in[0].astype(BF16), sg_ln_g, sg_ln_b, sg_w[0],
                          jnp.broadcast_to(sg_b[0][:, :, None], (SG_GROUPS, CHUNK, LANES)))
    b_out = _stickbreak(qkv, _later_sum_weights())
    xt = tail(0, xt, (a_out, b_out), ab_w_out[0], False)

    cos, sin = _rotary_tables()
    r = _ret_in(xt, mix_g, ret_w_in[0].astype(BF16), cos, sin)
    gated = _retention(r, *_retention_tables(), ret_gn_g)
    xt = tail(1, xt, (gated,), ret_w_out[0], True)
    return xt.reshape(BATCH, SEQ, D_MODEL)
```

```python
import functools

import jax
import jax.numpy as jnp
from jax import lax
from jax.experimental import pallas as pl
from jax.experimental.pallas import tpu as pltpu

F32 = jnp.float32
BF16 = jnp.bfloat16

D_MODEL = 1024
BATCH = 16
SEQ = 2048
TOKENS = BATCH * SEQ
DEPTH = 2
PLE_DIM = 256
CHUNK = 128
SG_GROUPS = 4
SG_WIDTH = 512
SB_HEADS = 8
SB_HEAD_DIM = 64
SB_WIDTH = 512
AB_IN = 2 * SG_WIDTH + 3 * SB_WIDTH
RET_HEADS = 4
RET_DK = 256
RET_DV = 512
RET_QK = 1024
RET_V = 2048
RET_IN = 6144
ROPE_BASE = 10000.0
D_FF = 2816
CONV_W = 3
RMS_EPS = 1e-6
LN_EPS = 1e-5
LOG2_E = 1.4426950408889634

LANES = 128
SUBLANES = 8
VMEM_LIMIT_BYTES = 56 * 1024 * 1024

ROW_TILE = 512
N_FF_UNITS = D_FF // LANES
PERM_PITCH = ROW_TILE // SUBLANES + SUBLANES
SB_TQ = 512
SB_TK = 128
SB_DEAD = 160.0
RET_CHUNK = 512
RET_HEADS_PER_STEP = 2


def _resident(shape):
    nd = len(shape)
    return pl.BlockSpec(shape, lambda *_: (0,) * nd, pipeline_mode=pl.Buffered(1))


def _layer_resident(shape, layer):
    nd = len(shape)
    return pl.BlockSpec((None,) + shape, lambda *_: (layer,) + (0,) * nd, pipeline_mode=pl.Buffered(1))


def _params(semantics):
    return pltpu.CompilerParams(dimension_semantics=semantics, vmem_limit_bytes=VMEM_LIMIT_BYTES)


def _rms(x, g):
    return x * lax.rsqrt(jnp.mean(x * x, axis=-1, keepdims=True) + RMS_EPS) * g


def _gelu(x):
    return 0.5 * x * (1.0 + lax.erf(x * (0.5 ** 0.5)))


def _sigmoid(x):
    return 1.0 / (1.0 + jnp.exp(-x))


def _mix0_in_kernel(x_ref, g_ref, win_ref, lng_ref, lnb_ref, ws_ref, bs_ref, a_ref, qkv_ref):
    h = _rms(x_ref[...], g_ref[...]).astype(BF16)
    uv = _gelu(jnp.dot(h, win_ref[:, :2 * SG_WIDTH], preferred_element_type=F32))
    qkv = jnp.dot(h, win_ref[:, 2 * SG_WIDTH:], preferred_element_type=F32)
    qkv_ref[:, :SB_WIDTH] = (qkv[:, :SB_WIDTH] * (SB_HEAD_DIM ** -0.5 * LOG2_E)).astype(BF16)
    qkv_ref[:, SB_WIDTH:] = qkv[:, SB_WIDTH:].astype(BF16)

    n_chunks = ROW_TILE // CHUNK
    row = lax.broadcasted_iota(jnp.int32, (CHUNK, CHUNK), 0)
    col = lax.broadcasted_iota(jnp.int32, (CHUNK, CHUNK), 1)
    for g in range(SG_GROUPS):
        gs = slice(g * LANES, (g + 1) * LANES)
        w = jnp.where(row >= col, ws_ref[g], 0.0).astype(BF16)
        vn = []
        for c in range(n_chunks):
            vf = uv[c * CHUNK:(c + 1) * CHUNK, SG_WIDTH + g * LANES:SG_WIDTH + (g + 1) * LANES]
            d = vf - jnp.mean(vf, axis=-1, keepdims=True)
            var = jnp.mean(d * d, axis=-1, keepdims=True)
            vn.append((d * lax.rsqrt(var + LN_EPS) * lng_ref[:, gs] + lnb_ref[:, gs]).astype(BF16))
        mixed = jnp.dot(w, jnp.concatenate(vn, axis=1), preferred_element_type=F32)
        for c in range(n_chunks):
            m = mixed[:, c * LANES:(c + 1) * LANES] + bs_ref[g]
            a_ref[c * CHUNK:(c + 1) * CHUNK, gs] = (uv[c * CHUNK:(c + 1) * CHUNK, gs] * m).astype(BF16)


def _mix0_in(x, g, win, lng, lnb, ws, bs):
    tm = ROW_TILE
    return pl.pallas_call(
        _mix0_in_kernel,
        grid=(TOKENS // tm,),
        in_specs=[
            pl.BlockSpec((tm, D_MODEL), lambda i: (i, 0)),
            _resident((1, D_MODEL)),
            _resident((D_MODEL, AB_IN)),
            _resident((1, SG_WIDTH)),
            _resident((1, SG_WIDTH)),
            _resident((SG_GROUPS, CHUNK, CHUNK)),
            _resident((SG_GROUPS, CHUNK, LANES)),
        ],
        out_specs=[
            pl.BlockSpec((tm, SG_WIDTH), lambda i: (i, 0)),
            pl.BlockSpec((tm, 3 * SB_WIDTH), lambda i: (i, 0)),
        ],
        out_shape=[
            jax.ShapeDtypeStruct((TOKENS, SG_WIDTH), BF16),
            jax.ShapeDtypeStruct((TOKENS, 3 * SB_WIDTH), BF16),
        ],
        compiler_params=_params(("arbitrary",)),
        name="mix0_in",
    )(x, g, win, lng, lnb, ws, bs)


def _sb_kernel(q_ref, k_ref, v_ref, w_ref, o_ref):
    i = pl.program_id(2)
    lane = lax.broadcasted_iota(jnp.int32, (1, LANES), 1)
    first_head = lane < SB_HEAD_DIM
    wcs = w_ref[...]

    def split_heads(blk):
        zero = jnp.zeros_like(blk)
        return jnp.concatenate([jnp.where(first_head, blk, zero), jnp.where(first_head, zero, blk)], axis=0)

    def softplus2(y):
        return jnp.maximum(y, 0.0) + jnp.log(1.0 + jnp.exp2(-jnp.abs(y))) * LOG2_E

    def block_total(later):
        return jnp.concatenate([jnp.broadcast_to(later[:, hd * SB_TK:hd * SB_TK + 1], (later.shape[0], SB_TK))
                                for hd in range(2)], axis=1)

    def step(j_hi, n_blocks, carry):
        c, acc = carry
        w = 2 * SB_TK
        kcat, vcat = [], []
        for u in range(n_blocks):
            r0 = pl.multiple_of((j_hi - u) * SB_TK, SB_TK)
            kcat.append(split_heads(k_ref[pl.ds(r0, SB_TK), :]))
            vcat.append(split_heads(v_ref[pl.ds(r0, SB_TK), :]))
        y = lax.dot_general(q_ref[...], jnp.concatenate(kcat, axis=0), (((1,), (1,)), ((), ())),
                            preferred_element_type=F32)
        sp = softplus2(y).astype(BF16)
        cs = [jnp.dot(sp[:, u * w:(u + 1) * w], wcs, preferred_element_type=F32) for u in range(n_blocks)]
        a = []
        for u in range(n_blocks):
            later = c + cs[u]
            a.append(jnp.exp2(y[:, u * w:(u + 1) * w] - later))
            c = block_total(later)
        acc = acc + jnp.dot(jnp.concatenate(a, axis=1).astype(BF16), jnp.concatenate(vcat, axis=0),
                            preferred_element_type=F32)
        return c, acc

    def diagonal(carry):
        c_all, acc_all = carry
        pre = []
        for d in reversed(range(blocks_per_q)):
            row0 = d * SB_TK
            r0 = pl.multiple_of((i * blocks_per_q + d) * SB_TK, SB_TK)
            kcat = split_heads(k_ref[pl.ds(r0, SB_TK), :])
            vcat = split_heads(v_ref[pl.ds(r0, SB_TK), :])
            y = lax.dot_general(q_ref[row0:, :], kcat, (((1,), (1,)), ((), ())), preferred_element_type=F32)
            row = row0 + lax.broadcasted_iota(jnp.int32, y.shape, 0)
            col = lax.broadcasted_iota(jnp.int32, y.shape, 1) & (SB_TK - 1)
            causal = (col + row0) < row
            cs = jnp.dot(jnp.where(causal, softplus2(y), 0.0).astype(BF16), wcs, preferred_element_type=F32)
            pre.append((row0, y, causal, cs, vcat))
        for row0, y, causal, cs, vcat in pre:
            later = c_all[row0:] + cs
            a = jnp.where(causal, jnp.exp2(y - later), 0.0)
            c = block_total(later)
            acc = acc_all[row0:] + jnp.dot(a.astype(BF16), vcat, preferred_element_type=F32)
            c_all = jnp.concatenate([c_all[:row0], c], axis=0) if row0 else c
            acc_all = jnp.concatenate([acc_all[:row0], acc], axis=0) if row0 else acc
        return c_all, acc_all

    blocks_per_q = SB_TQ // SB_TK
    carry = diagonal((jnp.zeros((SB_TQ, 2 * SB_TK), F32), jnp.zeros((SB_TQ, LANES), F32)))

    def live(c):
        per_row = jnp.minimum(c[:, 0:1], c[:, SB_TK:SB_TK + 1])
        return (jnp.min(per_row, axis=0, keepdims=True)[0, 0] < SB_DEAD).astype(jnp.int32)

    def group(state):
        n, _, c, acc = state
        c, acc = step((i - n) * blocks_per_q - 1, blocks_per_q, (c, acc))
        return n + 1, live(c), c, acc

    state = lax.while_loop(lambda st: (st[0] < i) & (st[1] > 0), group, (0, live(carry[0]), *carry))
    o_ref[...] = state[3].astype(BF16)


def _stickbreak(qkv, wcs):
    nq = SEQ // SB_TQ
    n_pairs = SB_WIDTH // LANES
    return pl.pallas_call(
        _sb_kernel,
        grid=(BATCH, n_pairs, nq),
        in_specs=[
            pl.BlockSpec((SB_TQ, LANES), lambda b, p, i: (b * nq + i, p)),
            pl.BlockSpec((SEQ, LANES), lambda b, p, i: (b, n_pairs + p)),
            pl.BlockSpec((SEQ, LANES), lambda b, p, i: (b, 2 * n_pairs + p)),
            _resident((2 * SB_TK, 2 * SB_TK)),
        ],
        out_specs=pl.BlockSpec((SB_TQ, LANES), lambda b, p, i: (b * nq + i, p)),
        out_shape=jax.ShapeDtypeStruct((TOKENS, SB_WIDTH), BF16),
        compiler_params=_params(("arbitrary", "arbitrary", "arbitrary")),
        name="stickbreak",
    )(qkv, qkv, qkv, wcs)


def _ffn_kernel(*refs, n_mix, final_norm):
    x_ref, mix_refs = refs[0], refs[1:1 + n_mix]
    (p_ref, wout_ref, fng_ref, wup_ref, cw_ref, cb_ref, wdn_ref, png_ref, wgate_ref, wproj_ref, ppg_ref,
     fin_ref, o_ref, perm_s, xi_s, h_s, prev_s, act_s) = refs[1 + n_mix:]
    tm = ROW_TILE
    n_blocks = tm // SUBLANES
    i = pl.program_id(0)
    seq_start = (i % (SEQ // tm)) == 0

    def interleaved(val):
        n_cb = val.shape[1] // LANES
        for cb in range(n_cb):
            for s in range(SUBLANES):
                perm_s[cb, s * PERM_PITCH:s * PERM_PITCH + n_blocks, :] = (
                    val[s * n_blocks:(s + 1) * n_blocks, cb * LANES:(cb + 1) * LANES])
        return jnp.concatenate(
            [jnp.concatenate([perm_s[cb, pl.ds(j, SUBLANES, stride=PERM_PITCH), :] for cb in range(n_cb)], axis=1)
             for j in range(n_blocks)], axis=0)

    def store_in_position_order(val):
        n_cb = val.shape[1] // LANES
        for j in range(n_blocks):
            for cb in range(n_cb):
                perm_s[cb, pl.ds(j, SUBLANES, stride=PERM_PITCH), :] = (
                    val[j * SUBLANES:(j + 1) * SUBLANES, cb * LANES:(cb + 1) * LANES])
        for cb in range(n_cb):
            for s in range(SUBLANES):
                o_ref[s * n_blocks:(s + 1) * n_blocks, cb * LANES:(cb + 1) * LANES] = (
                    perm_s[cb, s * PERM_PITCH:s * PERM_PITCH + n_blocks, :])

    x1 = x_ref[...]
    k0 = 0
    for m_ref in mix_refs:
        x1 = x1 + jnp.dot(m_ref[...], wout_ref[k0:k0 + m_ref.shape[1], :], preferred_element_type=F32)
        k0 += m_ref.shape[1]
    x1 = interleaved(x1)
    xi_s[...] = x1
    h_s[...] = _rms(x1, fng_ref[...]).astype(BF16)

    def gate_up(ref, u):
        return jnp.concatenate([ref[:, part * D_FF + u * LANES:part * D_FF + (u + 1) * LANES]
                                for part in range(2)], axis=1)

    sub = lax.broadcasted_iota(jnp.int32, (SUBLANES, 2 * LANES), 0)
    for u in range(N_FF_UNITS):
        a = jnp.dot(h_s[...], gate_up(wup_ref, u), preferred_element_type=F32)
        wrapped = []
        for back in range(1, CONV_W):
            last = a[tm - back * SUBLANES:tm - (back - 1) * SUBLANES, :]
            before = jnp.where(seq_start, 0.0, prev_s[u, back - 1])
            prev_s[u, back - 1] = last
            wrapped.append(jnp.where(sub == 0, pltpu.roll(before, 1, 0), pltpu.roll(last, 1, 0)))
        cw = gate_up(cw_ref, u)
        conv = gate_up(cb_ref, u) + cw[CONV_W - 1:CONV_W, :] * a
        for back in range(1, CONV_W):
            shifted = jnp.concatenate(wrapped[back - 1::-1] + [a[:tm - back * SUBLANES, :]], axis=0)
            conv = conv + cw[CONV_W - 1 - back:CONV_W - back, :] * shifted
        gate, upv = conv[:, :LANES], conv[:, LANES:]
        act_s[:, u * LANES:(u + 1) * LANES] = (gate * (1.0 + lax.erf(gate * (0.5 ** 0.5))) * upv).astype(BF16)

    x2 = xi_s[...] + 0.5 * jnp.dot(act_s[...], wdn_ref[...], preferred_element_type=F32)
    gate = _sigmoid(jnp.dot(_rms(x2, png_ref[...]).astype(BF16), wgate_ref[...], preferred_element_type=F32))
    pp = _rms(jnp.dot(interleaved(p_ref[...]).astype(BF16), wproj_ref[...], preferred_element_type=F32),
              ppg_ref[...])
    x3 = x2 + gate * pp
    if final_norm:
        x3 = _rms(x3, fin_ref[...])
    store_in_position_order(x3)


def _ffn_block(layer, x, mixes, p, wout, fng, wup, cw, cb, wdn, png, wgate, wproj, ppg, fin, *, final_norm):
    tm = ROW_TILE
    kmix = sum(m.shape[1] for m in mixes)
    lr = functools.partial(_layer_resident, layer=layer)
    return pl.pallas_call(
        functools.partial(_ffn_kernel, n_mix=len(mixes), final_norm=final_norm),
        grid=(TOKENS // tm,),
        in_specs=[pl.BlockSpec((tm, D_MODEL), lambda i: (i, 0))]
        + [pl.BlockSpec((tm, m.shape[1]), lambda i: (i, 0)) for m in mixes]
        + [
            pl.BlockSpec((None, tm, PLE_DIM), lambda i: (layer, i, 0)),
            _resident((kmix, D_MODEL)),
            lr((1, D_MODEL)),
            lr((D_MODEL, 2 * D_FF)),
            lr((CONV_W, 2 * D_FF)),
            lr((1, 2 * D_FF)),
            lr((D_FF, D_MODEL)),
            lr((1, D_MODEL)),
            lr((D_MODEL, D_MODEL)),
            lr((PLE_DIM, D_MODEL)),
            lr((1, D_MODEL)),
            _resident((1, D_MODEL)),
        ],
        out_specs=pl.BlockSpec((tm, D_MODEL), lambda i: (i, 0)),
        out_shape=jax.ShapeDtypeStruct((TOKENS, D_MODEL), F32),
        scratch_shapes=[
            pltpu.VMEM((D_MODEL // LANES, SUBLANES * PERM_PITCH, LANES), F32),
            pltpu.VMEM((tm, D_MODEL), F32),
            pltpu.VMEM((tm, D_MODEL), BF16),
            pltpu.VMEM((N_FF_UNITS, CONV_W - 1, SUBLANES, 2 * LANES), F32),
            pltpu.VMEM((tm, D_FF), BF16),
        ],
        compiler_params=_params(("arbitrary",)),
        name="ffn_block_final" if final_norm else "ffn_block",
    )(x, *mixes, p, wout, fng, wup, cw, cb, wdn, png, wgate, wproj, ppg, fin)


def _ret_in_kernel(x_ref, g_ref, w_ref, cos_ref, sin_ref, o_ref):
    h = _rms(x_ref[...], g_ref[...]).astype(BF16)
    cos = cos_ref[...]
    sin = sin_ref[...]
    half = RET_DK // 2
    blk = RET_QK
    for jb in range(RET_IN // blk):
        cols = slice(jb * blk, (jb + 1) * blk)
        pr = jnp.dot(h, w_ref[:, cols], preferred_element_type=F32)
        if jb < 2:
            scale = 1.0 if jb == 0 else RET_DK ** -0.5
            for hd in range(RET_HEADS):
                x1 = pr[:, hd * RET_DK:hd * RET_DK + half]
                x2 = pr[:, hd * RET_DK + half:(hd + 1) * RET_DK]
                base = jb * blk + hd * RET_DK
                o_ref[:, base:base + half] = ((x1 * cos - x2 * sin) * scale).astype(BF16)
                o_ref[:, base + half:base + RET_DK] = ((x2 * cos + x1 * sin) * scale).astype(BF16)
        elif jb < 2 + RET_V // blk:
            o_ref[:, cols] = pr.astype(BF16)
        else:
            o_ref[:, cols] = (pr * _sigmoid(pr)).astype(BF16)


def _ret_in(x, g, w, cos, sin):
    tm = ROW_TILE
    tiles_per_seq = SEQ // tm
    return pl.pallas_call(
        _ret_in_kernel,
        grid=(TOKENS // tm,),
        in_specs=[
            pl.BlockSpec((tm, D_MODEL), lambda i: (i, 0)),
            _layer_resident((1, D_MODEL), 1),
            _resident((D_MODEL, RET_IN)),
            pl.BlockSpec((tm, RET_DK // 2), lambda i: (i % tiles_per_seq, 0)),
            pl.BlockSpec((tm, RET_DK // 2), lambda i: (i % tiles_per_seq, 0)),
        ],
        out_specs=pl.BlockSpec((tm, RET_IN), lambda i: (i, 0)),
        out_shape=jax.ShapeDtypeStruct((TOKENS, RET_IN), BF16),
        compiler_params=_params(("arbitrary",)),
        name="ret_in",
    )(x, g, w, cos, sin)


def _retention_kernel(q_ref, k_ref, v_ref, sg_ref, dec_ref, zeta_ref, xi_ref, cd_ref, gn_ref, o_ref, state_s):
    state_s[...] = jnp.zeros_like(state_s)
    heads = range(RET_HEADS_PER_STEP)

    def chunk(c, _):
        rows = pl.ds(pl.multiple_of(c * RET_CHUNK, RET_CHUNK), RET_CHUNK)
        qk = lambda ref, hd: ref[rows, hd * RET_DK:(hd + 1) * RET_DK]
        vv = lambda ref, hd: ref[rows, hd * RET_DV:(hd + 1) * RET_DV]
        inner = [lax.dot_general(qk(q_ref, hd), qk(k_ref, hd), (((1,), (1,)), ((), ())),
                                 preferred_element_type=F32) * dec_ref[hd] for hd in heads]
        kv = [lax.dot_general((qk(k_ref, hd).astype(F32) * zeta_ref[hd]).astype(BF16), vv(v_ref, hd),
                              (((0,), (0,)), ((), ())), preferred_element_type=F32) for hd in heads]
        for hd in heads:
            st = state_s[hd]
            o = (jnp.dot(inner[hd].astype(BF16), vv(v_ref, hd), preferred_element_type=F32)
                 + jnp.dot(qk(q_ref, hd), st.astype(BF16), preferred_element_type=F32) * xi_ref[hd])
            state_s[hd] = st * cd_ref[hd] + kv[hd]
            d = o - jnp.mean(o, axis=-1, keepdims=True)
            var = jnp.mean(d * d, axis=-1, keepdims=True)
            on = d * lax.rsqrt(var + LN_EPS) * gn_ref[:, hd * RET_DV:(hd + 1) * RET_DV]
            o_ref[rows, hd * RET_DV:(hd + 1) * RET_DV] = (vv(sg_ref, hd).astype(F32) * on).astype(BF16)
        return 0

    lax.fori_loop(0, SEQ // RET_CHUNK, chunk, 0)


def _retention(r, dec, zeta, xi, cd, gn):
    n = RET_HEADS_PER_STEP
    groups = RET_HEADS // n
    table = lambda *tail: pl.BlockSpec((n,) + tail, lambda b, g: (g,) + (0,) * len(tail))
    return pl.pallas_call(
        _retention_kernel,
        grid=(BATCH, groups),
        in_specs=[
            pl.BlockSpec((SEQ, n * RET_DK), lambda b, g: (b, g)),
            pl.BlockSpec((SEQ, n * RET_DK), lambda b, g: (b, groups + g)),
            pl.BlockSpec((SEQ, n * RET_DV), lambda b, g: (b, groups + g)),
            pl.BlockSpec((SEQ, n * RET_DV), lambda b, g: (b, 2 * groups + g)),
            table(RET_CHUNK, RET_CHUNK),
            table(RET_CHUNK, 1),
            table(RET_CHUNK, 1),
            table(1, RET_DV),
            pl.BlockSpec((1, n * RET_DV), lambda b, g: (0, g)),
        ],
        out_specs=pl.BlockSpec((SEQ, n * RET_DV), lambda b, g: (b, g)),
        out_shape=jax.ShapeDtypeStruct((TOKENS, RET_V), BF16),
        scratch_shapes=[pltpu.VMEM((n, RET_DK, RET_DV), F32)],
        compiler_params=_params(("arbitrary", "arbitrary")),
        name="retention",
    )(r, r, r, r, dec, zeta, xi, cd, gn)


def _later_sum_weights():
    s_from = jnp.arange(2 * SB_TK)[:, None]
    s_to = jnp.arange(2 * SB_TK)[None, :]
    same_head = (s_from // SB_TK) == (s_to // SB_TK)
    return (same_head & (s_from >= s_to)).astype(BF16)


def _rotary_tables():
    half = RET_DK // 2
    inv = 1.0 / (ROPE_BASE ** (jnp.arange(half, dtype=F32) / half))
    ang = jnp.arange(SEQ, dtype=F32)[:, None] * inv[None, :]
    return jnp.cos(ang), jnp.sin(ang)


def _retention_tables():
    log_gamma = jnp.log(1.0 - 2.0 ** (-5.0 - jnp.arange(RET_HEADS, dtype=F32)))
    idx = jnp.arange(RET_CHUNK, dtype=F32)
    diff = idx[:, None] - idx[None, :]
    dec = jnp.where(diff[None] >= 0, jnp.exp(diff[None] * log_gamma[:, None, None]), 0.0)
    zeta = jnp.exp((RET_CHUNK - 1 - idx)[None, :] * log_gamma[:, None])[:, :, None]
    xi = jnp.exp((idx + 1.0)[None, :] * log_gamma[:, None])[:, :, None]
    cd = jnp.broadcast_to(jnp.exp(RET_CHUNK * log_gamma)[:, None, None], (RET_HEADS, 1, RET_DV))
    return dec, zeta, xi, cd


def kernel(x, p, mix_norm_g, ffn_norm_g, ple_norm_g, ab_w_in, sg_ln_g, sg_ln_b, sg_w, sg_b, ab_w_out,
           ret_w_in, ret_gn_g, ret_w_out, ffn_w_up, ffn_conv_w, ffn_conv_b, ffn_w_down,
           ple_w_gate, ple_w_proj, ple_post_g, final_norm_g):
    xt = x.reshape(TOKENS, D_MODEL)
    pt = p.reshape(DEPTH, TOKENS, PLE_DIM)
    rows = lambda t: t.reshape(t.shape[0], 1, -1)
    mix_g, ffn_g, ple_g, post_g = rows(mix_norm_g), rows(ffn_norm_g), rows(ple_norm_g), rows(ple_post_g)
    w_up, w_down = ffn_w_up.astype(BF16), ffn_w_down.astype(BF16)
    w_gate, w_proj = ple_w_gate.astype(BF16), ple_w_proj.astype(BF16)
    conv_b = rows(ffn_conv_b)
    fin = final_norm_g.reshape(1, D_MODEL)

    def tail(layer, xin, mixes, wout, final_norm):
        return _ffn_block(layer, xin, mixes, pt, wout.astype(BF16), ffn_g, w_up, ffn_conv_w, conv_b, w_down,
                          ple_g, w_gate, w_proj, post_g, fin, final_norm=final_norm)

    a_out, qkv = _mix0_in(xt, mix_g[0], ab_w_in[0].astype(BF16), sg_ln_g, sg_ln_b, sg_w[0],
                          jnp.broadcast_to(sg_b[0][:, :, None], (SG_GROUPS, CHUNK, LANES)))
    b_out = _stickbreak(qkv, _later_sum_weights())
    xt = tail(0, xt, (a_out, b_out), ab_w_out[0], False)

    cos, sin = _rotary_tables()
    r = _ret_in(xt, mix_g, ret_w_in[0].astype(BF16), cos, sin)
    gated = _retention(r, *_retention_tables(), ret_gn_g)
    xt = tail(1, xt, (gated,), ret_w_out[0], True)
    return xt.reshape(BATCH, SEQ, D_MODEL)
```

```python
import functools

import jax
import jax.numpy as jnp
from jax import lax
from jax.experimental import pallas as pl
from jax.experimental.pallas import tpu as pltpu

F32 = jnp.float32
BF16 = jnp.bfloat16

D_MODEL = 1024
BATCH = 16
SEQ = 2048
TOKENS = BATCH * SEQ
DEPTH = 2
PLE_DIM = 256
CHUNK = 128
SG_GROUPS = 4
SG_WIDTH = 512
SB_HEADS = 8
SB_HEAD_DIM = 64
SB_WIDTH = 512
AB_IN = 2 * SG_WIDTH + 3 * SB_WIDTH
RET_HEADS = 4
RET_DK = 256
RET_DV = 512
RET_QK = 1024
RET_V = 2048
RET_IN = 6144
ROPE_BASE = 10000.0
D_FF = 2816
CONV_W = 3
RMS_EPS = 1e-6
LN_EPS = 1e-5
LOG2_E = 1.4426950408889634

LANES = 128
SUBLANES = 8
VMEM_LIMIT_BYTES = 56 * 1024 * 1024

ROW_TILE = 512
N_FF_UNITS = D_FF // LANES
PERM_PITCH = ROW_TILE // SUBLANES + SUBLANES
SB_TQ = 512
SB_TK = 128
SB_DEAD = 160.0
RET_CHUNK = 512
RET_HEADS_PER_STEP = 2


def _resident(shape):
    nd = len(shape)
    return pl.BlockSpec(shape, lambda *_: (0,) * nd, pipeline_mode=pl.Buffered(1))


def _layer_resident(shape, layer):
    nd = len(shape)
    return pl.BlockSpec((None,) + shape, lambda *_: (layer,) + (0,) * nd, pipeline_mode=pl.Buffered(1))


def _params(semantics):
    return pltpu.CompilerParams(dimension_semantics=semantics, vmem_limit_bytes=VMEM_LIMIT_BYTES)


def _rms(x, g):
    return x * lax.rsqrt(jnp.mean(x * x, axis=-1, keepdims=True) + RMS_EPS) * g


def _gelu(x):
    return 0.5 * x * (1.0 + lax.erf(x * (0.5 ** 0.5)))


def _sigmoid(x):
    return 1.0 / (1.0 + jnp.exp(-x))


def _mix0_in_kernel(x_ref, g_ref, win_ref, lng_ref, lnb_ref, ws_ref, bs_ref, a_ref, qkv_ref):
    h = _rms(x_ref[...], g_ref[...]).astype(BF16)
    uv = _gelu(jnp.dot(h, win_ref[:, :2 * SG_WIDTH], preferred_element_type=F32))
    qkv = jnp.dot(h, win_ref[:, 2 * SG_WIDTH:], preferred_element_type=F32)
    qkv_ref[:, :SB_WIDTH] = (qkv[:, :SB_WIDTH] * (SB_HEAD_DIM ** -0.5 * LOG2_E)).astype(BF16)
    qkv_ref[:, SB_WIDTH:] = qkv[:, SB_WIDTH:].astype(BF16)

    n_chunks = ROW_TILE // CHUNK
    row = lax.broadcasted_iota(jnp.int32, (CHUNK, CHUNK), 0)
    col = lax.broadcasted_iota(jnp.int32, (CHUNK, CHUNK), 1)
    for g in range(SG_GROUPS):
        gs = slice(g * LANES, (g + 1) * LANES)
        w = jnp.where(row >= col, ws_ref[g], 0.0).astype(BF16)
        vn = []
        for c in range(n_chunks):
            vf = uv[c * CHUNK:(c + 1) * CHUNK, SG_WIDTH + g * LANES:SG_WIDTH + (g + 1) * LANES]
            d = vf - jnp.mean(vf, axis=-1, keepdims=True)
            var = jnp.mean(d * d, axis=-1, keepdims=True)
            vn.append((d * lax.rsqrt(var + LN_EPS) * lng_ref[:, gs] + lnb_ref[:, gs]).astype(BF16))
        mixed = jnp.dot(w, jnp.concatenate(vn, axis=1), preferred_element_type=F32)
        for c in range(n_chunks):
            m = mixed[:, c * LANES:(c + 1) * LANES] + bs_ref[g]
            a_ref[c * CHUNK:(c + 1) * CHUNK, gs] = (uv[c * CHUNK:(c + 1) * CHUNK, gs] * m).astype(BF16)


def _mix0_in(x, g, win, lng, lnb, ws, bs):
    tm = ROW_TILE
    return pl.pallas_call(
        _mix0_in_kernel,
        grid=(TOKENS // tm,),
        in_specs=[
            pl.BlockSpec((tm, D_MODEL), lambda i: (i, 0)),
            _resident((1, D_MODEL)),
            _resident((D_MODEL, AB_IN)),
            _resident((1, SG_WIDTH)),
            _resident((1, SG_WIDTH)),
            _resident((SG_GROUPS, CHUNK, CHUNK)),
            _resident((SG_GROUPS, CHUNK, LANES)),
        ],
        out_specs=[
            pl.BlockSpec((tm, SG_WIDTH), lambda i: (i, 0)),
            pl.BlockSpec((tm, 3 * SB_WIDTH), lambda i: (i, 0)),
        ],
        out_shape=[
            jax.ShapeDtypeStruct((TOKENS, SG_WIDTH), BF16),
            jax.ShapeDtypeStruct((TOKENS, 3 * SB_WIDTH), BF16),
        ],
        compiler_params=_params(("arbitrary",)),
        name="mix0_in",
    )(x, g, win, lng, lnb, ws, bs)


def _sb_kernel(q_ref, k_ref, v_ref, w_ref, o_ref):
    i = pl.program_id(2)
    lane = lax.broadcasted_iota(jnp.int32, (1, LANES), 1)
    first_head = lane < SB_HEAD_DIM
    wcs = w_ref[...]

    def split_heads(blk):
        zero = jnp.zeros_like(blk)
        return jnp.concatenate([jnp.where(first_head, blk, zero), jnp.where(first_head, zero, blk)], axis=0)

    def softplus2(y):
        return jnp.maximum(y, 0.0) + jnp.log(1.0 + jnp.exp2(-jnp.abs(y))) * LOG2_E

    def block_total(later):
        return jnp.concatenate([jnp.broadcast_to(later[:, hd * SB_TK:hd * SB_TK + 1], (later.shape[0], SB_TK))
                                for hd in range(2)], axis=1)

    def step(j_hi, n_blocks, carry, rows):
        c_all, acc_all = carry
        c, acc = c_all[:rows], acc_all[:rows]
        w = 2 * SB_TK
        kcat, vcat = [], []
        for u in range(n_blocks):
            r0 = pl.multiple_of((j_hi - u) * SB_TK, SB_TK)
            kcat.append(split_heads(k_ref[pl.ds(r0, SB_TK), :]))
            vcat.append(split_heads(v_ref[pl.ds(r0, SB_TK), :]))
        y = lax.dot_general(q_ref[:rows, :], jnp.concatenate(kcat, axis=0), (((1,), (1,)), ((), ())),
                            preferred_element_type=F32)
        sp = softplus2(y).astype(BF16)
        cs = [jnp.dot(sp[:, u * w:(u + 1) * w], wcs, preferred_element_type=F32) for u in range(n_blocks)]
        a = []
        for u in range(n_blocks):
            later = c + cs[u]
            a.append(jnp.exp2(y[:, u * w:(u + 1) * w] - later))
            c = block_total(later)
        acc = acc + jnp.dot(jnp.concatenate(a, axis=1).astype(BF16), jnp.concatenate(vcat, axis=0),
                            preferred_element_type=F32)
        if rows < SB_TQ:
            c = jnp.concatenate([c, c_all[rows:]], axis=0)
            acc = jnp.concatenate([acc, acc_all[rows:]], axis=0)
        return c, acc

    def diagonal(carry):
        c_all, acc_all = carry
        pre = []
        for d in reversed(range(blocks_per_q)):
            row0 = d * SB_TK
            r0 = pl.multiple_of((i * blocks_per_q + d) * SB_TK, SB_TK)
            kcat = split_heads(k_ref[pl.ds(r0, SB_TK), :])
            vcat = split_heads(v_ref[pl.ds(r0, SB_TK), :])
            y = lax.dot_general(q_ref[row0:, :], kcat, (((1,), (1,)), ((), ())), preferred_element_type=F32)
            row = row0 + lax.broadcasted_iota(jnp.int32, y.shape, 0)
            col = lax.broadcasted_iota(jnp.int32, y.shape, 1) & (SB_TK - 1)
            causal = (col + row0) < row
            cs = jnp.dot(jnp.where(causal, softplus2(y), 0.0).astype(BF16), wcs, preferred_element_type=F32)
            pre.append((row0, y, causal, cs, vcat))
        for row0, y, causal, cs, vcat in pre:
            later = c_all[row0:] + cs
            a = jnp.where(causal, jnp.exp2(y - later), 0.0)
            c = block_total(later)
            acc = acc_all[row0:] + jnp.dot(a.astype(BF16), vcat, preferred_element_type=F32)
            c_all = jnp.concatenate([c_all[:row0], c], axis=0) if row0 else c
            acc_all = jnp.concatenate([acc_all[:row0], acc], axis=0) if row0 else acc
        return c_all, acc_all

    blocks_per_q = SB_TQ // SB_TK
    carry = diagonal((jnp.zeros((SB_TQ, 2 * SB_TK), F32), jnp.zeros((SB_TQ, LANES), F32)))

    def live(c):
        per_row = jnp.minimum(c[:, 0:1], c[:, SB_TK:SB_TK + 1])
        return (jnp.min(per_row, axis=0, keepdims=True)[0, 0] < SB_DEAD).astype(jnp.int32)

    def group(state):
        n, _, c, acc = state
        j_hi = (i - n) * blocks_per_q - 1
        half = SB_TQ // 2
        c, acc = lax.cond(live(c[half:]) > 0,
                          lambda cr: step(j_hi, blocks_per_q, cr, SB_TQ),
                          lambda cr: step(j_hi, blocks_per_q, cr, half), (c, acc))
        return n + 1, live(c), c, acc

    state = lax.while_loop(lambda st: (st[0] < i) & (st[1] > 0), group, (0, live(carry[0]), *carry))
    o_ref[...] = state[3].astype(BF16)


def _stickbreak(qkv, wcs):
    nq = SEQ // SB_TQ
    n_pairs = SB_WIDTH // LANES
    return pl.pallas_call(
        _sb_kernel,
        grid=(BATCH, n_pairs, nq),
        in_specs=[
            pl.BlockSpec((SB_TQ, LANES), lambda b, p, i: (b * nq + i, p)),
            pl.BlockSpec((SEQ, LANES), lambda b, p, i: (b, n_pairs + p)),
            pl.BlockSpec((SEQ, LANES), lambda b, p, i: (b, 2 * n_pairs + p)),
            _resident((2 * SB_TK, 2 * SB_TK)),
        ],
        out_specs=pl.BlockSpec((SB_TQ, LANES), lambda b, p, i: (b * nq + i, p)),
        out_shape=jax.ShapeDtypeStruct((TOKENS, SB_WIDTH), BF16),
        compiler_params=_params(("arbitrary", "arbitrary", "arbitrary")),
        name="stickbreak",
    )(qkv, qkv, qkv, wcs)


def _ffn_kernel(*refs, n_mix, final_norm):
    x_ref, mix_refs = refs[0], refs[1:1 + n_mix]
    (p_ref, wout_ref, fng_ref, wup_ref, cw_ref, cb_ref, wdn_ref, png_ref, wgate_ref, wproj_ref, ppg_ref,
     fin_ref, o_ref, perm_s, xi_s, h_s, prev_s, act_s) = refs[1 + n_mix:]
    tm = ROW_TILE
    n_blocks = tm // SUBLANES
    i = pl.program_id(0)
    seq_start = (i % (SEQ // tm)) == 0

    def interleaved(val):
        n_cb = val.shape[1] // LANES
        for cb in range(n_cb):
            for s in range(SUBLANES):
                perm_s[cb, s * PERM_PITCH:s * PERM_PITCH + n_blocks, :] = (
                    val[s * n_blocks:(s + 1) * n_blocks, cb * LANES:(cb + 1) * LANES])
        return jnp.concatenate(
            [jnp.concatenate([perm_s[cb, pl.ds(j, SUBLANES, stride=PERM_PITCH), :] for cb in range(n_cb)], axis=1)
             for j in range(n_blocks)], axis=0)

    def store_in_position_order(val):
        n_cb = val.shape[1] // LANES
        for j in range(n_blocks):
            for cb in range(n_cb):
                perm_s[cb, pl.ds(j, SUBLANES, stride=PERM_PITCH), :] = (
                    val[j * SUBLANES:(j + 1) * SUBLANES, cb * LANES:(cb + 1) * LANES])
        for cb in range(n_cb):
            for s in range(SUBLANES):
                o_ref[s * n_blocks:(s + 1) * n_blocks, cb * LANES:(cb + 1) * LANES] = (
                    perm_s[cb, s * PERM_PITCH:s * PERM_PITCH + n_blocks, :])

    x1 = x_ref[...]
    k0 = 0
    for m_ref in mix_refs:
        x1 = x1 + jnp.dot(m_ref[...], wout_ref[k0:k0 + m_ref.shape[1], :], preferred_element_type=F32)
        k0 += m_ref.shape[1]
    x1 = interleaved(x1)
    xi_s[...] = x1
    h_s[...] = _rms(x1, fng_ref[...]).astype(BF16)

    def gate_up(ref, u):
        return jnp.concatenate([ref[:, part * D_FF + u * LANES:part * D_FF + (u + 1) * LANES]
                                for part in range(2)], axis=1)

    sub = lax.broadcasted_iota(jnp.int32, (SUBLANES, 2 * LANES), 0)
    for u in range(N_FF_UNITS):
        a = jnp.dot(h_s[...], gate_up(wup_ref, u), preferred_element_type=F32)
        wrapped = []
        for back in range(1, CONV_W):
            last = a[tm - back * SUBLANES:tm - (back - 1) * SUBLANES, :]
            before = jnp.where(seq_start, 0.0, prev_s[u, back - 1])
            prev_s[u, back - 1] = last
            wrapped.append(jnp.where(sub == 0, pltpu.roll(before, 1, 0), pltpu.roll(last, 1, 0)))
        cw = gate_up(cw_ref, u)
        conv = gate_up(cb_ref, u) + cw[CONV_W - 1:CONV_W, :] * a
        for back in range(1, CONV_W):
            shifted = jnp.concatenate(wrapped[back - 1::-1] + [a[:tm - back * SUBLANES, :]], axis=0)
            conv = conv + cw[CONV_W - 1 - back:CONV_W - back, :] * shifted
        gate, upv = conv[:, :LANES], conv[:, LANES:]
        act_s[:, u * LANES:(u + 1) * LANES] = (gate * (1.0 + lax.erf(gate * (0.5 ** 0.5))) * upv).astype(BF16)

    x2 = xi_s[...] + 0.5 * jnp.dot(act_s[...], wdn_ref[...], preferred_element_type=F32)
    gate = _sigmoid(jnp.dot(_rms(x2, png_ref[...]).astype(BF16), wgate_ref[...], preferred_element_type=F32))
    pp = _rms(jnp.dot(interleaved(p_ref[...]).astype(BF16), wproj_ref[...], preferred_element_type=F32),
              ppg_ref[...])
    x3 = x2 + gate * pp
    if final_norm:
        x3 = _rms(x3, fin_ref[...])
    store_in_position_order(x3)


def _ffn_block(layer, x, mixes, p, wout, fng, wup, cw, cb, wdn, png, wgate, wproj, ppg, fin, *, final_norm):
    tm = ROW_TILE
    kmix = sum(m.shape[1] for m in mixes)
    lr = functools.partial(_layer_resident, layer=layer)
    return pl.pallas_call(
        functools.partial(_ffn_kernel, n_mix=len(mixes), final_norm=final_norm),
        grid=(TOKENS // tm,),
        in_specs=[pl.BlockSpec((tm, D_MODEL), lambda i: (i, 0))]
        + [pl.BlockSpec((tm, m.shape[1]), lambda i: (i, 0)) for m in mixes]
        + [
            pl.BlockSpec((None, tm, PLE_DIM), lambda i: (layer, i, 0)),
            _resident((kmix, D_MODEL)),
            lr((1, D_MODEL)),
            lr((D_MODEL, 2 * D_FF)),
            lr((CONV_W, 2 * D_FF)),
            lr((1, 2 * D_FF)),
            lr((D_FF, D_MODEL)),
            lr((1, D_MODEL)),
            lr((D_MODEL, D_MODEL)),
            lr((PLE_DIM, D_MODEL)),
            lr((1, D_MODEL)),
            _resident((1, D_MODEL)),
        ],
        out_specs=pl.BlockSpec((tm, D_MODEL), lambda i: (i, 0)),
        out_shape=jax.ShapeDtypeStruct((TOKENS, D_MODEL), F32),
        scratch_shapes=[
            pltpu.VMEM((D_MODEL // LANES, SUBLANES * PERM_PITCH, LANES), F32),
            pltpu.VMEM((tm, D_MODEL), F32),
            pltpu.VMEM((tm, D_MODEL), BF16),
            pltpu.VMEM((N_FF_UNITS, CONV_W - 1, SUBLANES, 2 * LANES), F32),
            pltpu.VMEM((tm, D_FF), BF16),
        ],
        compiler_params=_params(("arbitrary",)),
        name="ffn_block_final" if final_norm else "ffn_block",
    )(x, *mixes, p, wout, fng, wup, cw, cb, wdn, png, wgate, wproj, ppg, fin)


def _ret_in_kernel(x_ref, g_ref, w_ref, cos_ref, sin_ref, o_ref):
    h = _rms(x_ref[...], g_ref[...]).astype(BF16)
    cos = cos_ref[...]
    sin = sin_ref[...]
    half = RET_DK // 2
    blk = RET_QK
    for jb in range(RET_IN // blk):
        cols = slice(jb * blk, (jb + 1) * blk)
        pr = jnp.dot(h, w_ref[:, cols], preferred_element_type=F32)
        if jb < 2:
            scale = 1.0 if jb == 0 else RET_DK ** -0.5
            for hd in range(RET_HEADS):
                x1 = pr[:, hd * RET_DK:hd * RET_DK + half]
                x2 = pr[:, hd * RET_DK + half:(hd + 1) * RET_DK]
                base = jb * blk + hd * RET_DK
                o_ref[:, base:base + half] = ((x1 * cos - x2 * sin) * scale).astype(BF16)
                o_ref[:, base + half:base + RET_DK] = ((x2 * cos + x1 * sin) * scale).astype(BF16)
        elif jb < 2 + RET_V // blk:
            o_ref[:, cols] = pr.astype(BF16)
        else:
            o_ref[:, cols] = (pr * _sigmoid(pr)).astype(BF16)


def _ret_in(x, g, w, cos, sin):
    tm = ROW_TILE
    tiles_per_seq = SEQ // tm
    return pl.pallas_call(
        _ret_in_kernel,
        grid=(TOKENS // tm,),
        in_specs=[
            pl.BlockSpec((tm, D_MODEL), lambda i: (i, 0)),
            _layer_resident((1, D_MODEL), 1),
            _resident((D_MODEL, RET_IN)),
            pl.BlockSpec((tm, RET_DK // 2), lambda i: (i % tiles_per_seq, 0)),
            pl.BlockSpec((tm, RET_DK // 2), lambda i: (i % tiles_per_seq, 0)),
        ],
        out_specs=pl.BlockSpec((tm, RET_IN), lambda i: (i, 0)),
        out_shape=jax.ShapeDtypeStruct((TOKENS, RET_IN), BF16),
        compiler_params=_params(("arbitrary",)),
        name="ret_in",
    )(x, g, w, cos, sin)


def _retention_kernel(q_ref, k_ref, v_ref, sg_ref, dec_ref, zeta_ref, xi_ref, cd_ref, gn_ref, o_ref, state_s):
    state_s[...] = jnp.zeros_like(state_s)
    heads = range(RET_HEADS_PER_STEP)

    def chunk(c, _):
        rows = pl.ds(pl.multiple_of(c * RET_CHUNK, RET_CHUNK), RET_CHUNK)
        qk = lambda ref, hd: ref[rows, hd * RET_DK:(hd + 1) * RET_DK]
        vv = lambda ref, hd: ref[rows, hd * RET_DV:(hd + 1) * RET_DV]
        inner = [lax.dot_general(qk(q_ref, hd), qk(k_ref, hd), (((1,), (1,)), ((), ())),
                                 preferred_element_type=F32) * dec_ref[hd] for hd in heads]
        kv = [lax.dot_general((qk(k_ref, hd).astype(F32) * zeta_ref[hd]).astype(BF16), vv(v_ref, hd),
                              (((0,), (0,)), ((), ())), preferred_element_type=F32) for hd in heads]
        for hd in heads:
            st = state_s[hd]
            o = (jnp.dot(inner[hd].astype(BF16), vv(v_ref, hd), preferred_element_type=F32)
                 + jnp.dot(qk(q_ref, hd), st.astype(BF16), preferred_element_type=F32) * xi_ref[hd])
            state_s[hd] = st * cd_ref[hd] + kv[hd]
            d = o - jnp.mean(o, axis=-1, keepdims=True)
            var = jnp.mean(d * d, axis=-1, keepdims=True)
            on = d * lax.rsqrt(var + LN_EPS) * gn_ref[:, hd * RET_DV:(hd + 1) * RET_DV]
            o_ref[rows, hd * RET_DV:(hd + 1) * RET_DV] = (vv(sg_ref, hd).astype(F32) * on).astype(BF16)
        return 0

    lax.fori_loop(0, SEQ // RET_CHUNK, chunk, 0)


def _retention(r, dec, zeta, xi, cd, gn):
    n = RET_HEADS_PER_STEP
    groups = RET_HEADS // n
    table = lambda *tail: pl.BlockSpec((n,) + tail, lambda b, g: (g,) + (0,) * len(tail))
    return pl.pallas_call(
        _retention_kernel,
        grid=(BATCH, groups),
        in_specs=[
            pl.BlockSpec((SEQ, n * RET_DK), lambda b, g: (b, g)),
            pl.BlockSpec((SEQ, n * RET_DK), lambda b, g: (b, groups + g)),
            pl.BlockSpec((SEQ, n * RET_DV), lambda b, g: (b, groups + g)),
            pl.BlockSpec((SEQ, n * RET_DV), lambda b, g: (b, 2 * groups + g)),
            table(RET_CHUNK, RET_CHUNK),
            table(RET_CHUNK, 1),
            table(RET_CHUNK, 1),
            table(1, RET_DV),
            pl.BlockSpec((1, n * RET_DV), lambda b, g: (0, g)),
        ],
        out_specs=pl.BlockSpec((SEQ, n * RET_DV), lambda b, g: (b, g)),
        out_shape=jax.ShapeDtypeStruct((TOKENS, RET_V), BF16),
        scratch_shapes=[pltpu.VMEM((n, RET_DK, RET_DV), F32)],
        compiler_params=_params(("arbitrary", "arbitrary")),
        name="retention",
    )(r, r, r, r, dec, zeta, xi, cd, gn)


def _later_sum_weights():
    s_from = jnp.arange(2 * SB_TK)[:, None]
    s_to = jnp.arange(2 * SB_TK)[None, :]
    same_head = (s_from // SB_TK) == (s_to // SB_TK)
    return (same_head & (s_from >= s_to)).astype(BF16)


def _rotary_tables():
    half = RET_DK // 2
    inv = 1.0 / (ROPE_BASE ** (jnp.arange(half, dtype=F32) / half))
    ang = jnp.arange(SEQ, dtype=F32)[:, None] * inv[None, :]
    return jnp.cos(ang), jnp.sin(ang)


def _retention_tables():
    log_gamma = jnp.log(1.0 - 2.0 ** (-5.0 - jnp.arange(RET_HEADS, dtype=F32)))
    idx = jnp.arange(RET_CHUNK, dtype=F32)
    diff = idx[:, None] - idx[None, :]
    dec = jnp.where(diff[None] >= 0, jnp.exp(diff[None] * log_gamma[:, None, None]), 0.0)
    zeta = jnp.exp((RET_CHUNK - 1 - idx)[None, :] * log_gamma[:, None])[:, :, None]
    xi = jnp.exp((idx + 1.0)[None, :] * log_gamma[:, None])[:, :, None]
    cd = jnp.broadcast_to(jnp.exp(RET_CHUNK * log_gamma)[:, None, None], (RET_HEADS, 1, RET_DV))
    return dec, zeta, xi, cd


def kernel(x, p, mix_norm_g, ffn_norm_g, ple_norm_g, ab_w_in, sg_ln_g, sg_ln_b, sg_w, sg_b, ab_w_out,
           ret_w_in, ret_gn_g, ret_w_out, ffn_w_up, ffn_conv_w, ffn_conv_b, ffn_w_down,
           ple_w_gate, ple_w_proj, ple_post_g, final_norm_g):
    xt = x.reshape(TOKENS, D_MODEL)
    pt = p.reshape(DEPTH, TOKENS, PLE_DIM)
    rows = lambda t: t.reshape(t.shape[0], 1, -1)
    mix_g, ffn_g, ple_g, post_g = rows(mix_norm_g), rows(ffn_norm_g), rows(ple_norm_g), rows(ple_post_g)
    w_up, w_down = ffn_w_up.astype(BF16), ffn_w_down.astype(BF16)
    w_gate, w_proj = ple_w_gate.astype(BF16), ple_w_proj.astype(BF16)
    conv_b = rows(ffn_conv_b)
    fin = final_norm_g.reshape(1, D_MODEL)

    def tail(layer, xin, mixes, wout, final_norm):
        return _ffn_block(layer, xin, mixes, pt, wout.astype(BF16), ffn_g, w_up, ffn_conv_w, conv_b, w_down,
                          ple_g, w_gate, w_proj, post_g, fin, final_norm=final_norm)

    a_out, qkv = _mix0_in(xt, mix_g[0], ab_w_in[0].astype(BF16), sg_ln_g, sg_ln_b, sg_w[0],
                          jnp.broadcast_to(sg_b[0][:, :, None], (SG_GROUPS, CHUNK, LANES)))
    b_out = _stickbreak(qkv, _later_sum_weights())
    xt = tail(0, xt, (a_out, b_out), ab_w_out[0], False)

    cos, sin = _rotary_tables()
    r = _ret_in(xt, mix_g, ret_w_in[0].astype(BF16), cos, sin)
    gated = _retention(r, *_retention_tables(), ret_gn_g)
    xt = tail(1, xt, (gated,), ret_w_out[0], True)
    return xt.reshape(BATCH, SEQ, D_MODEL)
```

```python
import functools

import jax
import jax.numpy as jnp
from jax import lax
from jax.experimental import pallas as pl
from jax.experimental.pallas import tpu as pltpu

F32 = jnp.float32
BF16 = jnp.bfloat16

D_MODEL = 1024
BATCH = 16
SEQ = 2048
TOKENS = BATCH * SEQ
DEPTH = 2
PLE_DIM = 256
CHUNK = 128
SG_GROUPS = 4
SG_WIDTH = 512
SB_HEADS = 8
SB_HEAD_DIM = 64
SB_WIDTH = 512
AB_IN = 2 * SG_WIDTH + 3 * SB_WIDTH
RET_HEADS = 4
RET_DK = 256
RET_DV = 512
RET_QK = 1024
RET_V = 2048
RET_IN = 6144
ROPE_BASE = 10000.0
D_FF = 2816
CONV_W = 3
RMS_EPS = 1e-6
LN_EPS = 1e-5
LOG2_E = 1.4426950408889634

LANES = 128
SUBLANES = 8
VMEM_LIMIT_BYTES = 56 * 1024 * 1024

ROW_TILE = 512
N_FF_UNITS = D_FF // LANES
PERM_PITCH = ROW_TILE // SUBLANES + SUBLANES
SB_TQ = 512
SB_TK = 128
SB_DEAD = 160.0
RET_CHUNK = 512
RET_HEADS_PER_STEP = 2


def _resident(shape):
    nd = len(shape)
    return pl.BlockSpec(shape, lambda *_: (0,) * nd, pipeline_mode=pl.Buffered(1))


def _layer_resident(shape, layer):
    nd = len(shape)
    return pl.BlockSpec((None,) + shape, lambda *_: (layer,) + (0,) * nd, pipeline_mode=pl.Buffered(1))


def _params(semantics):
    return pltpu.CompilerParams(dimension_semantics=semantics, vmem_limit_bytes=VMEM_LIMIT_BYTES)


def _rms(x, g):
    return x * lax.rsqrt(jnp.mean(x * x, axis=-1, keepdims=True) + RMS_EPS) * g


def _gelu(x):
    return 0.5 * x * (1.0 + lax.erf(x * (0.5 ** 0.5)))


def _sigmoid(x):
    return 1.0 / (1.0 + jnp.exp(-x))


def _mix0_in_kernel(x_ref, g_ref, win_ref, lng_ref, lnb_ref, ws_ref, bs_ref, a_ref, qkv_ref):
    h = _rms(x_ref[...], g_ref[...]).astype(BF16)
    uv = _gelu(jnp.dot(h, win_ref[:, :2 * SG_WIDTH], preferred_element_type=F32))
    qkv = jnp.dot(h, win_ref[:, 2 * SG_WIDTH:], preferred_element_type=F32)
    qkv_ref[:, :SB_WIDTH] = (qkv[:, :SB_WIDTH] * (SB_HEAD_DIM ** -0.5 * LOG2_E)).astype(BF16)
    qkv_ref[:, SB_WIDTH:] = qkv[:, SB_WIDTH:].astype(BF16)

    n_chunks = ROW_TILE // CHUNK
    row = lax.broadcasted_iota(jnp.int32, (CHUNK, CHUNK), 0)
    col = lax.broadcasted_iota(jnp.int32, (CHUNK, CHUNK), 1)
    for g in range(SG_GROUPS):
        gs = slice(g * LANES, (g + 1) * LANES)
        w = jnp.where(row >= col, ws_ref[g], 0.0).astype(BF16)
        vn = []
        for c in range(n_chunks):
            vf = uv[c * CHUNK:(c + 1) * CHUNK, SG_WIDTH + g * LANES:SG_WIDTH + (g + 1) * LANES]
            d = vf - jnp.mean(vf, axis=-1, keepdims=True)
            var = jnp.mean(d * d, axis=-1, keepdims=True)
            vn.append((d * lax.rsqrt(var + LN_EPS) * lng_ref[:, gs] + lnb_ref[:, gs]).astype(BF16))
        mixed = jnp.dot(w, jnp.concatenate(vn, axis=1), preferred_element_type=F32)
        for c in range(n_chunks):
            m = mixed[:, c * LANES:(c + 1) * LANES] + bs_ref[g]
            a_ref[c * CHUNK:(c + 1) * CHUNK, gs] = (uv[c * CHUNK:(c + 1) * CHUNK, gs] * m).astype(BF16)


def _mix0_in(x, g, win, lng, lnb, ws, bs):
    tm = ROW_TILE
    return pl.pallas_call(
        _mix0_in_kernel,
        grid=(TOKENS // tm,),
        in_specs=[
            pl.BlockSpec((tm, D_MODEL), lambda i: (i, 0)),
            _resident((1, D_MODEL)),
            _resident((D_MODEL, AB_IN)),
            _resident((1, SG_WIDTH)),
            _resident((1, SG_WIDTH)),
            _resident((SG_GROUPS, CHUNK, CHUNK)),
            _resident((SG_GROUPS, CHUNK, LANES)),
        ],
        out_specs=[
            pl.BlockSpec((tm, SG_WIDTH), lambda i: (i, 0)),
            pl.BlockSpec((tm, 3 * SB_WIDTH), lambda i: (i, 0)),
        ],
        out_shape=[
            jax.ShapeDtypeStruct((TOKENS, SG_WIDTH), BF16),
            jax.ShapeDtypeStruct((TOKENS, 3 * SB_WIDTH), BF16),
        ],
        compiler_params=_params(("arbitrary",)),
        name="mix0_in",
    )(x, g, win, lng, lnb, ws, bs)


def _sb_kernel(q_ref, k_ref, v_ref, w_ref, o_ref):
    def tile(i, carry):
        _sb_tile(i, q_ref, k_ref, v_ref, w_ref, o_ref)
        return carry

    lax.fori_loop(0, SEQ // SB_TQ, tile, 0)


def _sb_tile(i, q_ref, k_ref, v_ref, w_ref, o_ref):
    def q_rows(lo, hi):
        return q_ref[pl.ds(pl.multiple_of(i * SB_TQ + lo, SB_TK), hi - lo), :]

    lane = lax.broadcasted_iota(jnp.int32, (1, LANES), 1)
    first_head = lane < SB_HEAD_DIM
    wcs = w_ref[...]

    def split_heads(blk):
        zero = jnp.zeros_like(blk)
        return jnp.concatenate([jnp.where(first_head, blk, zero), jnp.where(first_head, zero, blk)], axis=0)

    def softplus2(y):
        return jnp.maximum(y, 0.0) + jnp.log(1.0 + jnp.exp2(-jnp.abs(y))) * LOG2_E

    def block_total(later):
        return jnp.concatenate([jnp.broadcast_to(later[:, hd * SB_TK:hd * SB_TK + 1], (later.shape[0], SB_TK))
                                for hd in range(2)], axis=1)

    def put_rows(full, part, lo, hi):
        return jnp.concatenate(([full[:lo]] if lo else []) + [part] + ([full[hi:]] if hi < SB_TQ else []), axis=0)

    def step(j_hi, n_blocks, carry, lo, hi):
        c_all, acc_all = carry
        c, acc = c_all[lo:hi], acc_all[lo:hi]
        w = 2 * SB_TK
        kcat, vcat = [], []
        for u in range(n_blocks):
            r0 = pl.multiple_of((j_hi - u) * SB_TK, SB_TK)
            kcat.append(split_heads(k_ref[pl.ds(r0, SB_TK), :]))
            vcat.append(split_heads(v_ref[pl.ds(r0, SB_TK), :]))
        y = lax.dot_general(q_rows(lo, hi), jnp.concatenate(kcat, axis=0), (((1,), (1,)), ((), ())),
                            preferred_element_type=F32)
        sp = softplus2(y).astype(BF16)
        cs = [jnp.dot(sp[:, u * w:(u + 1) * w], wcs, preferred_element_type=F32) for u in range(n_blocks)]
        a = []
        for u in range(n_blocks):
            later = c + cs[u]
            a.append(jnp.exp2(y[:, u * w:(u + 1) * w] - later))
            c = block_total(later)
        acc = acc + jnp.dot(jnp.concatenate(a, axis=1).astype(BF16), jnp.concatenate(vcat, axis=0),
                            preferred_element_type=F32)
        return put_rows(c_all, c, lo, hi), put_rows(acc_all, acc, lo, hi)

    blocks_per_q = SB_TQ // SB_TK
    half = SB_TQ // 2

    def live(c):
        per_row = jnp.minimum(c[:, 0:1], c[:, SB_TK:SB_TK + 1])
        return (jnp.min(per_row, axis=0, keepdims=True)[0, 0] < SB_DEAD).astype(jnp.int32)

    def diagonal(with_previous_group):
        pre = []
        for d in reversed(range(blocks_per_q)):
            lo = d * SB_TK
            r0 = pl.multiple_of((i * blocks_per_q + d) * SB_TK, SB_TK)
            y = lax.dot_general(q_rows(lo, SB_TQ), split_heads(k_ref[pl.ds(r0, SB_TK), :]),
                                (((1,), (1,)), ((), ())), preferred_element_type=F32)
            row = lo + lax.broadcasted_iota(jnp.int32, y.shape, 0)
            col = lax.broadcasted_iota(jnp.int32, y.shape, 1) & (SB_TK - 1)
            causal = (col + lo) < row
            cs = jnp.dot(jnp.where(causal, softplus2(y), 0.0).astype(BF16), wcs, preferred_element_type=F32)
            pre.append((lo, SB_TQ, y, causal, cs, split_heads(v_ref[pl.ds(r0, SB_TK), :])))
        if with_previous_group:
            for u in range(blocks_per_q):
                r0 = pl.multiple_of((i * blocks_per_q - 1 - u) * SB_TK, SB_TK)
                y = lax.dot_general(q_rows(0, half), split_heads(k_ref[pl.ds(r0, SB_TK), :]),
                                    (((1,), (1,)), ((), ())), preferred_element_type=F32)
                cs = jnp.dot(softplus2(y).astype(BF16), wcs, preferred_element_type=F32)
                pre.append((0, half, y, None, cs, split_heads(v_ref[pl.ds(r0, SB_TK), :])))
        c_all, acc_all = jnp.zeros((SB_TQ, 2 * SB_TK), F32), jnp.zeros((SB_TQ, LANES), F32)
        for lo, hi, y, causal, cs, vcat in pre:
            later = c_all[lo:hi] + cs
            a = jnp.exp2(y - later)
            if causal is not None:
                a = jnp.where(causal, a, 0.0)
            acc = acc_all[lo:hi] + jnp.dot(a.astype(BF16), vcat, preferred_element_type=F32)
            c_all = put_rows(c_all, block_total(later), lo, hi)
            acc_all = put_rows(acc_all, acc, lo, hi)
        return c_all, acc_all

    def later_tile():
        carry = diagonal(True)
        return lax.cond(live(carry[0][half:]) > 0,
                        lambda cr: step(i * blocks_per_q - 1, blocks_per_q, cr, half, SB_TQ),
                        lambda cr: cr, carry)

    c, acc = lax.cond(i > 0, later_tile, lambda: diagonal(False))

    def group(state):
        n, _, c, acc = state
        j_hi = (i - n) * blocks_per_q - 1
        c, acc = lax.cond(live(c[half:]) > 0,
                          lambda cr: step(j_hi, blocks_per_q, cr, 0, SB_TQ),
                          lambda cr: step(j_hi, blocks_per_q, cr, 0, half), (c, acc))
        return n + 1, live(c), c, acc

    state = lax.while_loop(lambda st: (st[0] < i) & (st[1] > 0), group, (jnp.minimum(i, 1), live(c), c, acc))
    o_ref[pl.ds(pl.multiple_of(i * SB_TQ, SB_TQ), SB_TQ), :] = state[3].astype(BF16)


def _stickbreak(qkv, wcs):
    n_pairs = SB_WIDTH // LANES
    return pl.pallas_call(
        _sb_kernel,
        grid=(BATCH, n_pairs),
        in_specs=[
            pl.BlockSpec((SEQ, LANES), lambda b, p: (b, p)),
            pl.BlockSpec((SEQ, LANES), lambda b, p: (b, n_pairs + p)),
            pl.BlockSpec((SEQ, LANES), lambda b, p: (b, 2 * n_pairs + p)),
            _resident((2 * SB_TK, 2 * SB_TK)),
        ],
        out_specs=pl.BlockSpec((SEQ, LANES), lambda b, p: (b, p)),
        out_shape=jax.ShapeDtypeStruct((TOKENS, SB_WIDTH), BF16),
        compiler_params=_params(("arbitrary", "arbitrary")),
        name="stickbreak",
    )(qkv, qkv, qkv, wcs)


def _ffn_kernel(*refs, n_mix, final_norm):
    x_ref, mix_refs = refs[0], refs[1:1 + n_mix]
    (p_ref, wout_ref, fng_ref, wup_ref, cw_ref, cb_ref, wdn_ref, png_ref, wgate_ref, wproj_ref, ppg_ref,
     fin_ref, o_ref, perm_s, xi_s, h_s, prev_s, act_s, pp_s) = refs[1 + n_mix:]
    tm = ROW_TILE
    n_blocks = tm // SUBLANES
    i = pl.program_id(0)
    seq_start = (i % (SEQ // tm)) == 0

    def interleaved(val):
        n_cb = val.shape[1] // LANES
        for cb in range(n_cb):
            for s in range(SUBLANES):
                perm_s[cb, s * PERM_PITCH:s * PERM_PITCH + n_blocks, :] = (
                    val[s * n_blocks:(s + 1) * n_blocks, cb * LANES:(cb + 1) * LANES])
        return jnp.concatenate(
            [jnp.concatenate([perm_s[cb, pl.ds(j, SUBLANES, stride=PERM_PITCH), :] for cb in range(n_cb)], axis=1)
             for j in range(n_blocks)], axis=0)

    def store_in_position_order(val):
        n_cb = val.shape[1] // LANES
        for j in range(n_blocks):
            for cb in range(n_cb):
                perm_s[cb, pl.ds(j, SUBLANES, stride=PERM_PITCH), :] = (
                    val[j * SUBLANES:(j + 1) * SUBLANES, cb * LANES:(cb + 1) * LANES])
        for cb in range(n_cb):
            for s in range(SUBLANES):
                o_ref[s * n_blocks:(s + 1) * n_blocks, cb * LANES:(cb + 1) * LANES] = (
                    perm_s[cb, s * PERM_PITCH:s * PERM_PITCH + n_blocks, :])

    pp_s[...] = _rms(jnp.dot(interleaved(p_ref[...]).astype(BF16), wproj_ref[...], preferred_element_type=F32),
                     ppg_ref[...])

    x1 = x_ref[...]
    k0 = 0
    for m_ref in mix_refs:
        x1 = x1 + jnp.dot(m_ref[...], wout_ref[k0:k0 + m_ref.shape[1], :], preferred_element_type=F32)
        k0 += m_ref.shape[1]
    x1 = interleaved(x1)
    xi_s[...] = x1
    h_s[...] = _rms(x1, fng_ref[...]).astype(BF16)

    def gate_up(ref, u):
        return jnp.concatenate([ref[:, part * D_FF + u * LANES:part * D_FF + (u + 1) * LANES]
                                for part in range(2)], axis=1)

    sub = lax.broadcasted_iota(jnp.int32, (SUBLANES, 2 * LANES), 0)
    for u in range(N_FF_UNITS):
        a = jnp.dot(h_s[...], gate_up(wup_ref, u), preferred_element_type=F32)
        wrapped = []
        for back in range(1, CONV_W):
            last = a[tm - back * SUBLANES:tm - (back - 1) * SUBLANES, :]
            before = jnp.where(seq_start, 0.0, prev_s[u, back - 1])
            prev_s[u, back - 1] = last
            wrapped.append(jnp.where(sub == 0, pltpu.roll(before, 1, 0), pltpu.roll(last, 1, 0)))
        cw = gate_up(cw_ref, u)
        conv = gate_up(cb_ref, u) + cw[CONV_W - 1:CONV_W, :] * a
        for back in range(1, CONV_W):
            shifted = jnp.concatenate(wrapped[back - 1::-1] + [a[:tm - back * SUBLANES, :]], axis=0)
            conv = conv + cw[CONV_W - 1 - back:CONV_W - back, :] * shifted
        gate, upv = conv[:, :LANES], conv[:, LANES:]
        act_s[:, u * LANES:(u + 1) * LANES] = (gate * (1.0 + lax.erf(gate * (0.5 ** 0.5))) * upv).astype(BF16)

    x2 = xi_s[...] + 0.5 * jnp.dot(act_s[...], wdn_ref[...], preferred_element_type=F32)
    gate = _sigmoid(jnp.dot(_rms(x2, png_ref[...]).astype(BF16), wgate_ref[...], preferred_element_type=F32))
    x3 = x2 + gate * pp_s[...]
    if final_norm:
        x3 = _rms(x3, fin_ref[...])
    store_in_position_order(x3)


def _ffn_block(layer, x, mixes, p, wout, fng, wup, cw, cb, wdn, png, wgate, wproj, ppg, fin, *, final_norm):
    tm = ROW_TILE
    kmix = sum(m.shape[1] for m in mixes)
    lr = functools.partial(_layer_resident, layer=layer)
    return pl.pallas_call(
        functools.partial(_ffn_kernel, n_mix=len(mixes), final_norm=final_norm),
        grid=(TOKENS // tm,),
        in_specs=[pl.BlockSpec((tm, D_MODEL), lambda i: (i, 0))]
        + [pl.BlockSpec((tm, m.shape[1]), lambda i: (i, 0)) for m in mixes]
        + [
            pl.BlockSpec((None, tm, PLE_DIM), lambda i: (layer, i, 0)),
            _resident((kmix, D_MODEL)),
            lr((1, D_MODEL)),
            lr((D_MODEL, 2 * D_FF)),
            lr((CONV_W, 2 * D_FF)),
            lr((1, 2 * D_FF)),
            lr((D_FF, D_MODEL)),
            lr((1, D_MODEL)),
            lr((D_MODEL, D_MODEL)),
            lr((PLE_DIM, D_MODEL)),
            lr((1, D_MODEL)),
            _resident((1, D_MODEL)),
        ],
        out_specs=pl.BlockSpec((tm, D_MODEL), lambda i: (i, 0)),
        out_shape=jax.ShapeDtypeStruct((TOKENS, D_MODEL), F32),
        scratch_shapes=[
            pltpu.VMEM((D_MODEL // LANES, SUBLANES * PERM_PITCH, LANES), F32),
            pltpu.VMEM((tm, D_MODEL), F32),
            pltpu.VMEM((tm, D_MODEL), BF16),
            pltpu.VMEM((N_FF_UNITS, CONV_W - 1, SUBLANES, 2 * LANES), F32),
            pltpu.VMEM((tm, D_FF), BF16),
            pltpu.VMEM((tm, D_MODEL), F32),
        ],
        compiler_params=_params(("arbitrary",)),
        name="ffn_block_final" if final_norm else "ffn_block",
    )(x, *mixes, p, wout, fng, wup, cw, cb, wdn, png, wgate, wproj, ppg, fin)


def _ret_in_kernel(x_ref, g_ref, w_ref, cos_ref, sin_ref, o_ref):
    h = _rms(x_ref[...], g_ref[...]).astype(BF16)
    cos = cos_ref[...]
    sin = sin_ref[...]
    half = RET_DK // 2
    blk = RET_QK
    for jb in range(RET_IN // blk):
        cols = slice(jb * blk, (jb + 1) * blk)
        pr = jnp.dot(h, w_ref[:, cols], preferred_element_type=F32)
        if jb < 2:
            scale = 1.0 if jb == 0 else RET_DK ** -0.5
            for hd in range(RET_HEADS):
                x1 = pr[:, hd * RET_DK:hd * RET_DK + half]
                x2 = pr[:, hd * RET_DK + half:(hd + 1) * RET_DK]
                base = jb * blk + hd * RET_DK
                o_ref[:, base:base + half] = ((x1 * cos - x2 * sin) * scale).astype(BF16)
                o_ref[:, base + half:base + RET_DK] = ((x2 * cos + x1 * sin) * scale).astype(BF16)
        elif jb < 2 + RET_V // blk:
            o_ref[:, cols] = pr.astype(BF16)
        else:
            o_ref[:, cols] = (pr * _sigmoid(pr)).astype(BF16)


def _ret_in(x, g, w, cos, sin):
    tm = ROW_TILE
    tiles_per_seq = SEQ // tm
    return pl.pallas_call(
        _ret_in_kernel,
        grid=(TOKENS // tm,),
        in_specs=[
            pl.BlockSpec((tm, D_MODEL), lambda i: (i, 0)),
            _layer_resident((1, D_MODEL), 1),
            _resident((D_MODEL, RET_IN)),
            pl.BlockSpec((tm, RET_DK // 2), lambda i: (i % tiles_per_seq, 0)),
            pl.BlockSpec((tm, RET_DK // 2), lambda i: (i % tiles_per_seq, 0)),
        ],
        out_specs=pl.BlockSpec((tm, RET_IN), lambda i: (i, 0)),
        out_shape=jax.ShapeDtypeStruct((TOKENS, RET_IN), BF16),
        compiler_params=_params(("arbitrary",)),
        name="ret_in",
    )(x, g, w, cos, sin)


def _retention_kernel(q_ref, k_ref, v_ref, sg_ref, dec_ref, zeta_ref, xi_ref, cd_ref, gn_ref, o_ref, state_s):
    state_s[...] = jnp.zeros_like(state_s)
    heads = range(RET_HEADS_PER_STEP)

    def chunk(c, _):
        rows = pl.ds(pl.multiple_of(c * RET_CHUNK, RET_CHUNK), RET_CHUNK)
        qk = lambda ref, hd: ref[rows, hd * RET_DK:(hd + 1) * RET_DK]
        vv = lambda ref, hd: ref[rows, hd * RET_DV:(hd + 1) * RET_DV]
        inner = [lax.dot_general(qk(q_ref, hd), qk(k_ref, hd), (((1,), (1,)), ((), ())),
                                 preferred_element_type=F32) * dec_ref[hd] for hd in heads]
        o = [jnp.dot(inner[hd].astype(BF16), vv(v_ref, hd), preferred_element_type=F32)
             + jnp.dot(qk(q_ref, hd), state_s[hd].astype(BF16), preferred_element_type=F32) * xi_ref[hd]
             for hd in heads]
        for hd in heads:
            d = o[hd] - jnp.mean(o[hd], axis=-1, keepdims=True)
            var = jnp.mean(d * d, axis=-1, keepdims=True)
            on = d * lax.rsqrt(var + LN_EPS) * gn_ref[:, hd * RET_DV:(hd + 1) * RET_DV]
            o_ref[rows, hd * RET_DV:(hd + 1) * RET_DV] = (vv(sg_ref, hd).astype(F32) * on).astype(BF16)
        for hd in heads:
            kz = (qk(k_ref, hd).astype(F32) * zeta_ref[hd]).astype(BF16)
            state_s[hd] = state_s[hd] * cd_ref[hd] + lax.dot_general(
                kz, vv(v_ref, hd), (((0,), (0,)), ((), ())), preferred_element_type=F32)
        return 0

    lax.fori_loop(0, SEQ // RET_CHUNK, chunk, 0)


def _retention(r, dec, zeta, xi, cd, gn):
    n = RET_HEADS_PER_STEP
    groups = RET_HEADS // n
    table = lambda *tail: pl.BlockSpec((n,) + tail, lambda b, g: (g,) + (0,) * len(tail))
    return pl.pallas_call(
        _retention_kernel,
        grid=(BATCH, groups),
        in_specs=[
            pl.BlockSpec((SEQ, n * RET_DK), lambda b, g: (b, g)),
            pl.BlockSpec((SEQ, n * RET_DK), lambda b, g: (b, groups + g)),
            pl.BlockSpec((SEQ, n * RET_DV), lambda b, g: (b, groups + g)),
            pl.BlockSpec((SEQ, n * RET_DV), lambda b, g: (b, 2 * groups + g)),
            table(RET_CHUNK, RET_CHUNK),
            table(RET_CHUNK, 1),
            table(RET_CHUNK, 1),
            table(1, RET_DV),
            pl.BlockSpec((1, n * RET_DV), lambda b, g: (0, g)),
        ],
        out_specs=pl.BlockSpec((SEQ, n * RET_DV), lambda b, g: (b, g)),
        out_shape=jax.ShapeDtypeStruct((TOKENS, RET_V), BF16),
        scratch_shapes=[pltpu.VMEM((n, RET_DK, RET_DV), F32)],
        compiler_params=_params(("arbitrary", "arbitrary")),
        name="retention",
    )(r, r, r, r, dec, zeta, xi, cd, gn)


def _later_sum_weights():
    s_from = jnp.arange(2 * SB_TK)[:, None]
    s_to = jnp.arange(2 * SB_TK)[None, :]
    same_head = (s_from // SB_TK) == (s_to // SB_TK)
    return (same_head & (s_from >= s_to)).astype(BF16)


def _rotary_tables():
    half = RET_DK // 2
    inv = 1.0 / (ROPE_BASE ** (jnp.arange(half, dtype=F32) / half))
    ang = jnp.arange(SEQ, dtype=F32)[:, None] * inv[None, :]
    return jnp.cos(ang), jnp.sin(ang)


def _retention_tables():
    log_gamma = jnp.log(1.0 - 2.0 ** (-5.0 - jnp.arange(RET_HEADS, dtype=F32)))
    idx = jnp.arange(RET_CHUNK, dtype=F32)
    diff = idx[:, None] - idx[None, :]
    dec = jnp.where(diff[None] >= 0, jnp.exp(diff[None] * log_gamma[:, None, None]), 0.0)
    zeta = jnp.exp((RET_CHUNK - 1 - idx)[None, :] * log_gamma[:, None])[:, :, None]
    xi = jnp.exp((idx + 1.0)[None, :] * log_gamma[:, None])[:, :, None]
    cd = jnp.broadcast_to(jnp.exp(RET_CHUNK * log_gamma)[:, None, None], (RET_HEADS, 1, RET_DV))
    return dec, zeta, xi, cd


def kernel(x, p, mix_norm_g, ffn_norm_g, ple_norm_g, ab_w_in, sg_ln_g, sg_ln_b, sg_w, sg_b, ab_w_out,
           ret_w_in, ret_gn_g, ret_w_out, ffn_w_up, ffn_conv_w, ffn_conv_b, ffn_w_down,
           ple_w_gate, ple_w_proj, ple_post_g, final_norm_g):
    xt = x.reshape(TOKENS, D_MODEL)
    pt = p.reshape(DEPTH, TOKENS, PLE_DIM)
    rows = lambda t: t.reshape(t.shape[0], 1, -1)
    mix_g, ffn_g, ple_g, post_g = rows(mix_norm_g), rows(ffn_norm_g), rows(ple_norm_g), rows(ple_post_g)
    w_up, w_down = ffn_w_up.astype(BF16), ffn_w_down.astype(BF16)
    w_gate, w_proj = ple_w_gate.astype(BF16), ple_w_proj.astype(BF16)
    conv_b = rows(ffn_conv_b)
    fin = final_norm_g.reshape(1, D_MODEL)

    def tail(layer, xin, mixes, wout, final_norm):
        return _ffn_block(layer, xin, mixes, pt, wout.astype(BF16), ffn_g, w_up, ffn_conv_w, conv_b, w_down,
                          ple_g, w_gate, w_proj, post_g, fin, final_norm=final_norm)

    a_out, qkv = _mix0_in(xt, mix_g[0], ab_w_in[0].astype(BF16), sg_ln_g, sg_ln_b, sg_w[0],
                          jnp.broadcast_to(sg_b[0][:, :, None], (SG_GROUPS, CHUNK, LANES)))
    b_out = _stickbreak(qkv, _later_sum_weights())
    xt = tail(0, xt, (a_out, b_out), ab_w_out[0], False)

    cos, sin = _rotary_tables()
    r = _ret_in(xt, mix_g, ret_w_in[0].astype(BF16), cos, sin)
    gated = _retention(r, *_retention_tables(), ret_gn_g)
    xt = tail(1, xt, (gated,), ret_w_out[0], True)
    return xt.reshape(BATCH, SEQ, D_MODEL)
```
